```python
import math
import jax
import jax.numpy as jnp
from jax import lax
import numpy as np

D_MODEL = 1024
BATCH = 2
SEQ = 16384
DEPTH = 1
DEC_BATCH = 16
DEC_SEQ = 16
PAST_LEN = 4096

CHUNK = 64
Q_BLOCK = 128
A_HEADS = 4
A_HEAD_DIM = 128
IDX_HEADS = 8
IDX_DIM = 64
TOPK_MAX = 256
B_HEADS = 4
B_QK_DIM = 64
B_V_DIM = 128
D_MIX = A_HEADS * A_HEAD_DIM + B_HEADS * B_V_DIM
D_FF = 2816
FFN_RES = 0.5
N_BUCKETS = 32
MAX_DISTANCE = 128
N_BIAS_HEADS = A_HEADS + B_HEADS
N_SUB = 3
RMS_EPS = 1e-6
SUBLN_EPS = 1e-5
IN_SIZES = (
    A_HEADS * A_HEAD_DIM,
    A_HEADS * A_HEAD_DIM,
    A_HEADS * A_HEAD_DIM,
    IDX_HEADS * IDX_DIM,
    IDX_DIM,
    IDX_HEADS,
    B_HEADS * 2 * B_QK_DIM,
    B_HEADS * 2 * B_QK_DIM,
    B_HEADS * B_V_DIM,
)
D_IN = 3656

kernel_name = "chunk_causal_dsa_diffattn_macaron_encoder_step"


def _rmsnorm(x, g, eps=RMS_EPS):
    xf = x.astype(jnp.float32)
    y = xf * lax.rsqrt(jnp.mean(xf * xf, axis=-1, keepdims=True) + eps)
    return (y * g.astype(jnp.float32)).astype(x.dtype)


def _rel_bucket(rel):
    nb = N_BUCKETS // 2
    max_exact = nb // 2
    ret = jnp.where(rel > 0, nb, 0)
    n = jnp.abs(rel)
    nf = jnp.maximum(n, 1).astype(jnp.float32)
    large = max_exact + (jnp.log(nf / max_exact) / math.log(MAX_DISTANCE / max_exact) * (nb - max_exact)).astype(jnp.int32)
    large = jnp.minimum(large, nb - 1)
    return ret + jnp.where(n < max_exact, n, large)


def _chunk_end(pos):
    return (pos // CHUNK + 1) * CHUNK


def _swiglu(h, w_gate, w_up, w_down):
    return (jax.nn.silu(h @ w_gate) * (h @ w_up)) @ w_down


def _split_in(z):
    outs, off = [], 0
    for n in IN_SIZES:
        outs.append(z[..., off:off + n])
        off += n
    return outs


def _dsa_block(q, q_idx, w_idx, k, v, k_idx, qpos, kpos, n_sel, bias_tab):
    dot = jnp.einsum('bqhd,bkd->bqhk', q_idx, k_idx).astype(jnp.float32) * (IDX_DIM ** -0.5)
    iscore = jnp.einsum('bqh,bqhk->bqk', w_idx.astype(jnp.float32) * (IDX_HEADS ** -0.5), jax.nn.relu(dot))
    qend = _chunk_end(qpos)
    admit = kpos[None, :] < qend[:, None]
    iscore = jnp.where(admit[None], iscore, -jnp.inf)
    _, sel = lax.top_k(iscore, n_sel)
    spos = kpos[sel]
    valid = spos < qend[None, :, None]
    gather = jax.vmap(lambda a, i: a[i])
    kg = gather(k, sel)
    vg = gather(v, sel)
    logits = jnp.einsum('bqhd,bqjhd->bhqj', q, kg).astype(jnp.float32) * (A_HEAD_DIM ** -0.5)
    bias = bias_tab[_rel_bucket(spos - qpos[None, :, None])][..., :A_HEADS]
    logits = logits + jnp.moveaxis(bias, -1, 1).astype(jnp.float32)
    logits = jnp.where(valid[:, None], logits, -jnp.inf)
    p = jax.nn.softmax(logits, axis=-1).astype(v.dtype)
    return jnp.einsum('bhqj,bqjhd->bqhd', p, vg)


def _diff_block(q, k, v, qpos, kpos, lam, bias_tab):
    s = jnp.einsum('bqhcd,bkhcd->bchqk', q, k).astype(jnp.float32) * (B_QK_DIM ** -0.5)
    bias = bias_tab[_rel_bucket(kpos[None, :] - qpos[:, None])][..., A_HEADS:]
    s = s + jnp.transpose(bias, (2, 0, 1))[None, None].astype(jnp.float32)
    admit = kpos[None, :] < _chunk_end(qpos)[:, None]
    s = jnp.where(admit, s, -jnp.inf)
    p = jax.nn.softmax(s, axis=-1)
    a = (p[:, 0] - lam * p[:, 1]).astype(v.dtype)
    return jnp.einsum('bhqk,bkhd->bqhd', a, v)


def _token_mix(h, past, pos0, layer_idx, w_in, w_out, lam_q1, lam_k1, lam_q2, lam_k2, g_subln, bias_tab):
    B, T, _ = h.shape
    q_a, k_a, v_a, q_idx, k_idx, w_idx, q_b, k_b, v_b = _split_in(h @ w_in)
    q_a = q_a.reshape(B, T, A_HEADS, A_HEAD_DIM)
    k_a = k_a.reshape(B, T, A_HEADS, A_HEAD_DIM)
    v_a = v_a.reshape(B, T, A_HEADS, A_HEAD_DIM)
    q_idx = q_idx.reshape(B, T, IDX_HEADS, IDX_DIM)
    q_b = q_b.reshape(B, T, B_HEADS, 2, B_QK_DIM)
    k_b = k_b.reshape(B, T, B_HEADS, 2, B_QK_DIM)
    v_b = v_b.reshape(B, T, B_HEADS, B_V_DIM)
    new_rows = (k_a, v_a, k_idx, k_b.reshape(B, T, B_HEADS, 2 * B_QK_DIM), v_b)
    if past is None:
        K_a, V_a, K_idx, K_b, V_b = k_a, v_a, k_idx, k_b, v_b
    else:
        pk_a, pv_a, pk_idx, pk_b, pv_b = past
        P = pk_a.shape[1]
        K_a = jnp.concatenate([pk_a, k_a], axis=1)
        V_a = jnp.concatenate([pv_a, v_a], axis=1)
        K_idx = jnp.concatenate([pk_idx, k_idx], axis=1)
        K_b = jnp.concatenate([pk_b.reshape(B, P, B_HEADS, 2, B_QK_DIM), k_b], axis=1)
        V_b = jnp.concatenate([pv_b, v_b], axis=1)
    L = K_a.shape[1]
    kpos = jnp.arange(L, dtype=jnp.int32)
    qpos = pos0 + jnp.arange(T, dtype=jnp.int32)
    n_sel = min(TOPK_MAX, L // 4)
    lam_init = 0.8 - 0.6 * math.exp(-0.3 * layer_idx)
    lam = (jnp.exp(jnp.sum(lam_q1.astype(jnp.float32) * lam_k1.astype(jnp.float32)))
           - jnp.exp(jnp.sum(lam_q2.astype(jnp.float32) * lam_k2.astype(jnp.float32))) + lam_init)

    def attend(blk):
        qa, qi, wi, qb, qp = blk
        oa = _dsa_block(qa, qi, wi, K_a, V_a, K_idx, qp, kpos, n_sel, bias_tab)
        ob = _diff_block(qb, K_b, V_b, qp, kpos, lam, bias_tab)
        return oa, ob

    if T % Q_BLOCK == 0:
        nb = T // Q_BLOCK

        def to_blocks(a):
            return jnp.moveaxis(a.reshape((B, nb, Q_BLOCK) + a.shape[2:]), 1, 0)

        def from_blocks(a):
            return jnp.moveaxis(a, 0, 1).reshape((B, T) + a.shape[3:])

        oa, ob = lax.map(attend, (to_blocks(q_a), to_blocks(q_idx), to_blocks(w_idx), to_blocks(q_b),
                                  qpos.reshape(nb, Q_BLOCK)))
        oa, ob = from_blocks(oa), from_blocks(ob)
    else:
        oa, ob = attend((q_a, q_idx, w_idx, q_b, qpos))
    ob = _rmsnorm(ob, g_subln, SUBLN_EPS) * (1.0 - lam_init)
    o = jnp.concatenate([oa.reshape(B, T, -1), ob.reshape(B, T, -1)], axis=-1)
    return o @ w_out, new_rows


def _layer(x, c, past, pos0, layer_idx, lw, bias_tab):
    (w_ada, b_ada, g_pre, g_post, f1g, f1u, f1d, f2g, f2u, f2d,
     w_in, w_out, lq1, lk1, lq2, lk2, g_subln) = lw
    B = x.shape[0]
    mod = (jax.nn.silu(c) @ w_ada + b_ada).reshape(B, N_SUB, 3, 1, D_MODEL)

    def pre(x, i):
        return _rmsnorm(x, g_pre[i]) * (1.0 + mod[:, i, 1]) + mod[:, i, 0]

    def post(x, out, i, res_w):
        return x + res_w * mod[:, i, 2] * _rmsnorm(out, g_post[i])

    x = post(x, _swiglu(pre(x, 0), f1g, f1u, f1d), 0, FFN_RES)
    mix, rows = _token_mix(pre(x, 1), past, pos0, layer_idx, w_in, w_out, lq1, lk1, lq2, lk2, g_subln, bias_tab)
    x = post(x, mix, 1, 1.0)
    x = post(x, _swiglu(pre(x, 2), f2g, f2u, f2d), 2, FFN_RES)
    return x, rows


def setup_inputs(seed: int = 0) -> dict:
    key = jax.random.key(seed)
    ks = jax.random.split(key, 32)
    D = D_MODEL

    def nrm(k, shape, s):
        return jax.random.normal(k, shape, jnp.float32) * s

    return {
        "x_prompt": nrm(ks[0], (BATCH, SEQ, D), 1.0),
        "x_sample": nrm(ks[1], (DEC_BATCH, DEC_SEQ, D), 1.0),
        "cache_a_k": nrm(ks[2], (DEPTH, DEC_BATCH, PAST_LEN, A_HEADS, A_HEAD_DIM), 1.0),
        "cache_a_v": nrm(ks[3], (DEPTH, DEC_BATCH, PAST_LEN, A_HEADS, A_HEAD_DIM), 1.0),
        "cache_a_kidx": nrm(ks[4], (DEPTH, DEC_BATCH, PAST_LEN, IDX_DIM), 1.0),
        "cache_b_k": nrm(ks[5], (DEPTH, DEC_BATCH, PAST_LEN, B_HEADS, 2 * B_QK_DIM), 1.0),
        "cache_b_v": nrm(ks[6], (DEPTH, DEC_BATCH, PAST_LEN, B_HEADS, B_V_DIM), 1.0),
        "c_prompt": nrm(ks[7], (BATCH, D), 1.0),
        "c_sample": nrm(ks[8], (DEC_BATCH, D), 1.0),
        "rel_bias": nrm(ks[9], (N_BUCKETS, N_BIAS_HEADS), 0.5),
        "w_ada": nrm(ks[10], (DEPTH, D, N_SUB * 3 * D), 0.5 * D ** -0.5),
        "b_ada": nrm(ks[11], (DEPTH, N_SUB * 3 * D), 0.01),
        "g_pre": 1.0 + nrm(ks[12], (DEPTH, N_SUB, D), 0.05),
        "g_post": 1.0 + nrm(ks[13], (DEPTH, N_SUB, D), 0.05),
        "ffn1_w_gate": nrm(ks[14], (DEPTH, D, D_FF), D ** -0.5),
        "ffn1_w_up": nrm(ks[15], (DEPTH, D, D_FF), D ** -0.5),
        "ffn1_w_down": nrm(ks[16], (DEPTH, D_FF, D), D_FF ** -0.5),
        "ffn2_w_gate": nrm(ks[17], (DEPTH, D, D_FF), D ** -0.5),
        "ffn2_w_up": nrm(ks[18], (DEPTH, D, D_FF), D ** -0.5),
        "ffn2_w_down": nrm(ks[19], (DEPTH, D_FF, D), D_FF ** -0.5),
        "w_in": nrm(ks[20], (DEPTH, D, D_IN), D ** -0.5),
        "w_out": nrm(ks[21], (DEPTH, D_MIX, D), D_MIX ** -0.5),
        "lambda_q1": nrm(ks[22], (DEPTH, B_QK_DIM), 0.1),
        "lambda_k1": nrm(ks[23], (DEPTH, B_QK_DIM), 0.1),
        "lambda_q2": nrm(ks[24], (DEPTH, B_QK_DIM), 0.1),
        "lambda_k2": nrm(ks[25], (DEPTH, B_QK_DIM), 0.1),
        "g_subln": 1.0 + nrm(ks[26], (DEPTH, B_V_DIM), 0.05),
    }


def reference(x_prompt, x_sample, cache_a_k, cache_a_v, cache_a_kidx, cache_b_k, cache_b_v,
              c_prompt, c_sample, rel_bias, w_ada, b_ada, g_pre, g_post,
              ffn1_w_gate, ffn1_w_up, ffn1_w_down, ffn2_w_gate, ffn2_w_up, ffn2_w_down,
              w_in, w_out, lambda_q1, lambda_k1, lambda_q2, lambda_k2, g_subln):
    past_len = cache_a_k.shape[2]
    yp, ys = x_prompt, x_sample
    rows_p, rows_s = [], []
    for l in range(DEPTH):
        lw = (w_ada[l], b_ada[l], g_pre[l], g_post[l],
              ffn1_w_gate[l], ffn1_w_up[l], ffn1_w_down[l],
              ffn2_w_gate[l], ffn2_w_up[l], ffn2_w_down[l],
              w_in[l], w_out[l], lambda_q1[l], lambda_k1[l], lambda_q2[l], lambda_k2[l], g_subln[l])
        yp, rp = _layer(yp, c_prompt, None, 0, l, lw, rel_bias)
        past = (cache_a_k[l], cache_a_v[l], cache_a_kidx[l], cache_b_k[l], cache_b_v[l])
        ys, rs = _layer(ys, c_sample, past, past_len, l, lw, rel_bias)
        rows_p.append(rp)
        rows_s.append(rs)

    def st(rows, j):
        return jnp.stack([r[j] for r in rows], axis=0)

    return (yp, ys,
            st(rows_p, 0), st(rows_p, 1), st(rows_p, 2), st(rows_p, 3), st(rows_p, 4),
            st(rows_s, 0), st(rows_s, 1), st(rows_s, 2), st(rows_s, 3), st(rows_s, 4))
```

```python
import functools
import math

import jax
import jax.numpy as jnp
from jax import lax
from jax.experimental import pallas as pl
from jax.experimental.pallas import tpu as pltpu

D_MODEL = 1024
CHUNK = 64
A_HEADS = 4
A_HEAD_DIM = 128
IDX_HEADS = 8
IDX_DIM = 64
TOPK_MAX = 256
B_HEADS = 4
B_QK_DIM = 64
B_V_DIM = 128
D_MIX = A_HEADS * A_HEAD_DIM + B_HEADS * B_V_DIM
D_FF = 2816
FFN_RES = 0.5
N_BUCKETS = 32
MAX_DISTANCE = 128
N_SUB = 3
RMS_EPS = 1e-6
SUBLN_EPS = 1e-5
IN_SIZES = (512, 512, 512, 512, 64, 8, 512, 512, 512)

LANES = 128
VMEM_LIMIT_BYTES = 56 * 1024 * 1024

FF_CHUNK = 256
TOKEN_TILE = 512
TQ = 256
TK = 256
SAMPLE_TK = 1024
NEW_PAD = 128

NEG = -1e30
A_SCALE = A_HEAD_DIM ** -0.5
B_SCALE = B_QK_DIM ** -0.5
IDX_SCALE = IDX_DIM ** -0.5
IDX_W_SCALE = IDX_HEADS ** -0.5
INT_MIN = -(2 ** 31)

SEG_QA, SEG_QI, SEG_QB1, SEG_QB2, SEG_KA, SEG_VA, SEG_KW, SEG_KB, SEG_VB = (
    (0, 512), (512, 1024), (1536, 512), (2048, 512), (2560, 512), (3072, 512), (3584, 128), (3712, 512), (4224, 512))
N_PACKED = 4736

_NT = (((1,), (1,)), ((), ()))
_F32 = jnp.float32
_BF16 = jnp.bfloat16


def _cparams(sem):
    return pltpu.CompilerParams(dimension_semantics=sem, vmem_limit_bytes=VMEM_LIMIT_BYTES)


def _resident(shape, index_map):
    return pl.BlockSpec(shape, index_map, pipeline_mode=pl.Buffered(1))


def _ada_kernel(c_ref, w_ref, b_ref, o_ref):
    c = c_ref[...]
    s = c * jax.nn.sigmoid(c)
    o_ref[...] = jnp.dot(s, w_ref[...], precision=lax.Precision.HIGHEST,
                         preferred_element_type=_F32) + b_ref[...]


def _ada(c, w_ada, b_ada):
    rows, d = c.shape
    n = w_ada.shape[1]
    tn = 1024
    return pl.pallas_call(
        _ada_kernel,
        grid=(n // tn,),
        in_specs=[pl.BlockSpec((rows, d), lambda j: (0, 0)),
                  pl.BlockSpec((d, tn), lambda j: (0, j)),
                  pl.BlockSpec((1, tn), lambda j: (0, j))],
        out_specs=pl.BlockSpec((rows, tn), lambda j: (0, j)),
        out_shape=jax.ShapeDtypeStruct((rows, n), _F32),
        compiler_params=_cparams(("parallel",)),
        name="ada_mod",
    )(c, w_ada, b_ada.reshape(1, n))


def _rms(x, g, eps):
    return x * lax.rsqrt(jnp.mean(x * x, axis=-1, keepdims=True) + eps) * g


def _mod_specs(per_token, tm, d):
    if per_token:
        return pl.BlockSpec((None, tm, d), lambda b, t: (b, t, 0))
    return pl.BlockSpec((None, 1, d), lambda b, t: (b, 0, 0))


def _ffn_kernel(x_ref, sh_ref, sc_ref, gt_ref, gpre_ref, gpost_ref, wg_ref, wu_ref, wd_ref,
                o_ref, h_ref, acc_ref, *, n_chunks, res_w):
    x = x_ref[...]
    h = _rms(x, gpre_ref[...], RMS_EPS) * (1.0 + sc_ref[...]) + sh_ref[...]
    h_ref[...] = h.astype(_BF16)
    acc_ref[...] = jnp.zeros_like(acc_ref)

    def body(c, carry):
        hb = h_ref[...]
        g = jnp.dot(hb, wg_ref[c], preferred_element_type=_F32)
        u = jnp.dot(hb, wu_ref[c], preferred_element_type=_F32)
        a = (g * jax.nn.sigmoid(g)) * u
        acc_ref[...] += jnp.dot(a.astype(_BF16), wd_ref[c], preferred_element_type=_F32)
        return carry

    lax.fori_loop(0, n_chunks, body, 0)
    y = _rms(acc_ref[...], gpost_ref[...], RMS_EPS)
    o_ref[...] = x + res_w * gt_ref[...] * y


def _ffn(x, shift, scale, gate, g_pre, g_post, wg3, wu3, wd3, *, per_token, res_w):
    bsz, t, d = x.shape
    tm = min(TOKEN_TILE, t)
    n_chunks = wg3.shape[0]
    tok = pl.BlockSpec((None, tm, d), lambda b, i: (b, i, 0))
    vec = pl.BlockSpec((1, d), lambda b, i: (0, 0))
    mod = _mod_specs(per_token, tm, d)
    return pl.pallas_call(
        functools.partial(_ffn_kernel, n_chunks=n_chunks, res_w=res_w),
        grid=(bsz, t // tm),
        in_specs=[tok, mod, mod, mod, vec, vec,
                  _resident(wg3.shape, lambda b, i: (0, 0, 0)),
                  _resident(wu3.shape, lambda b, i: (0, 0, 0)),
                  _resident(wd3.shape, lambda b, i: (0, 0, 0))],
        out_specs=tok,
        out_shape=jax.ShapeDtypeStruct(x.shape, _F32),
        scratch_shapes=[pltpu.VMEM((tm, d), _BF16), pltpu.VMEM((tm, d), _F32)],
        compiler_params=_cparams(("parallel", "parallel")),
        name="ffn",
    )(x, shift, scale, gate, g_pre.reshape(1, d), g_post.reshape(1, d), wg3, wu3, wd3)


def _inproj_kernel(x_ref, sh_ref, sc_ref, gpre_ref, w_ref, *out_refs, transposed):
    if transposed:
        (qa_ref, qi_ref, qb1_ref, qb2_ref, ka_ref, va_ref, kw_ref, kb_ref, vb_ref,
         kabf_ref, vat_ref, kwbf_ref, kwt_ref, kbbf_ref, vbt_ref) = out_refs
    else:
        (qa_ref, qi_ref, qb1_ref, qb2_ref, ka_ref, va_ref, kw_ref, kb_ref, vb_ref,
         kabf_ref, vabf_ref, kbbf_ref, vbbf_ref) = out_refs
    h = _rms(x_ref[...], gpre_ref[...], RMS_EPS) * (1.0 + sc_ref[...]) + sh_ref[...]
    hb = h.astype(_BF16)

    def proj(seg):
        off, width = seg
        return jnp.dot(hb, w_ref[:, off:off + width], preferred_element_type=_F32)

    qa_ref[...] = (proj(SEG_QA) * A_SCALE).astype(_BF16)
    qi_ref[...] = (proj(SEG_QI) * IDX_SCALE).astype(_BF16)
    qb1_ref[...] = (proj(SEG_QB1) * B_SCALE).astype(_BF16)
    qb2_ref[...] = (proj(SEG_QB2) * B_SCALE).astype(_BF16)
    ka = proj(SEG_KA)
    ka_ref[...] = ka
    kabf_ref[...] = ka.astype(_BF16)
    kb = proj(SEG_KB)
    kb_ref[...] = kb
    kbbf_ref[...] = kb.astype(_BF16)
    kw = proj(SEG_KW)
    kw_ref[...] = kw
    va = proj(SEG_VA)
    va_ref[...] = va
    vb = proj(SEG_VB)
    vb_ref[...] = vb
    if transposed:
        kwbf_ref[...] = kw.astype(_BF16)
        kwt_ref[...] = kw.T
        vat_ref[...] = va.T.astype(_BF16)
        vbt_ref[...] = vb.T.astype(_BF16)
    else:
        vabf_ref[...] = va.astype(_BF16)
        vbbf_ref[...] = vb.astype(_BF16)


def _inproj(x, shift, scale, g_pre, w_packed, *, per_token, transposed):
    bsz, t, d = x.shape
    tm = min(TOKEN_TILE, t)

    def tok(width, dtype):
        return (pl.BlockSpec((None, tm, width), lambda b, i: (b, i, 0)),
                jax.ShapeDtypeStruct((bsz, t, width), dtype))

    def tr(width, dtype):
        return (pl.BlockSpec((None, width, tm), lambda b, i: (b, 0, i)),
                jax.ShapeDtypeStruct((bsz, width, t), dtype))

    outs = [tok(512, _BF16), tok(1024, _BF16), tok(512, _BF16), tok(512, _BF16),
            tok(512, _F32), tok(512, _F32), tok(LANES, _F32), tok(512, _F32), tok(512, _F32)]
    if transposed:
        outs += [tok(512, _BF16), tr(512, _BF16), tok(LANES, _BF16), tr(LANES, _F32),
                 tok(512, _BF16), tr(512, _BF16)]
    else:
        outs += [tok(512, _BF16), tok(512, _BF16), tok(512, _BF16), tok(512, _BF16)]
    mod = _mod_specs(per_token, tm, d)
    return pl.pallas_call(
        functools.partial(_inproj_kernel, transposed=transposed),
        grid=(bsz, t // tm),
        in_specs=[pl.BlockSpec((None, tm, d), lambda b, i: (b, i, 0)), mod, mod,
                  pl.BlockSpec((1, d), lambda b, i: (0, 0)),
                  _resident(w_packed.shape, lambda b, i: (0, 0))],
        out_specs=[o[0] for o in outs],
        out_shape=[o[1] for o in outs],
        compiler_params=_cparams(("parallel", "parallel")),
        name="inproj",
    )(x, shift, scale, g_pre.reshape(1, d), w_packed)


def _bucket_thresholds():
    nb = N_BUCKETS // 2
    max_exact = nb // 2
    ratio = MAX_DISTANCE / max_exact
    out = []
    for j in range(1, nb - max_exact):
        out.append(int(math.ceil(max_exact * ratio ** (j / (nb - max_exact)) - 1e-9)))
    return nb, max_exact, tuple(out)


def _bias_tile_kernel(tab_ref, o_ref, *, q_on_cols, q0, k0, n_keys):
    hh = pl.program_id(0)
    shape = o_ref.shape
    r = lax.broadcasted_iota(jnp.int32, shape, 0)
    c = lax.broadcasted_iota(jnp.int32, shape, 1)
    qpos = q0 + (c if q_on_cols else r)
    kpos = k0 + (r if q_on_cols else c)
    rel = kpos - qpos
    n = jnp.abs(rel)
    nb, max_exact, thresholds = _bucket_thresholds()
    large = jnp.full(shape, max_exact, jnp.int32)
    for th in thresholds:
        large = large + jnp.where(n >= th, 1, 0)
    bucket = jnp.where(rel > 0, nb, 0) + jnp.where(n < max_exact, n, large)
    val = jnp.zeros(shape, _F32)
    for bkt in range(N_BUCKETS):
        val = jnp.where(bucket == bkt, tab_ref[hh, bkt], val)
    val = val - tab_ref[hh, nb - 1]
    chunk_end = (lax.shift_right_arithmetic(qpos, int(math.log2(CHUNK))) + 1) * CHUNK
    admit = (kpos < chunk_end) & (kpos >= 0) & (kpos < n_keys)
    o_ref[...] = jnp.where(admit, val, NEG)


def _bias_tiles(rel_bias_t, rows, cols, *, q_on_cols, q0, k0, n_keys):
    n_heads = rel_bias_t.shape[0]
    return pl.pallas_call(
        functools.partial(_bias_tile_kernel, q_on_cols=q_on_cols, q0=q0, k0=k0, n_keys=n_keys),
        grid=(n_heads,),
        in_specs=[pl.BlockSpec(memory_space=pltpu.SMEM)],
        out_specs=pl.BlockSpec((None, rows, cols), lambda h: (h, 0, 0)),
        out_shape=jax.ShapeDtypeStruct((n_heads, rows, cols), _F32),
        compiler_params=_cparams(("parallel",)),
        name="bias_tiles",
    )(rel_bias_t)


def _order_key(score):
    bits = lax.bitcast_convert_type(score, jnp.int32)
    return bits ^ (lax.shift_right_arithmetic(bits, 31) & 0x7FFFFFFF)


def _select_threshold(count_ge, shape, n_sel):
    def body(it, t_u):
        bit = lax.shift_left(jnp.int32(1), 31 - it)
        cand_u = t_u | bit
        cnt = count_ge(cand_u ^ INT_MIN)
        return jnp.where(cnt >= n_sel, cand_u, t_u)

    t_u = lax.fori_loop(0, 32, body, jnp.zeros(shape, jnp.int32))
    return t_u ^ INT_MIN


def _lambda(lam_ref, lam_init):
    l1 = jnp.sum(lam_ref[0:1, :] * lam_ref[1:2, :], axis=-1, keepdims=True)
    l2 = jnp.sum(lam_ref[2:3, :] * lam_ref[3:4, :], axis=-1, keepdims=True)
    return jnp.exp(l1) - jnp.exp(l2) + lam_init


def _attn_prompt_kernel(qa_ref, qi_ref, qb1_ref, qb2_ref, kwt_ref, kw_ref, bias_ref, lam_ref, gsub_ref,
                        ka_hbm, vat_hbm, kb_hbm, vbt_hbm, o_ref,
                        s_ref, kabuf, vabuf, kbbuf, vbbuf, sem, acc_ref, m_ref, l_ref,
                        *, n_sel, lam_init):
    b = pl.program_id(0)
    i = pl.program_id(1)
    n_streams = A_HEADS + 2 * B_HEADS

    def copies(c, slot):
        k0 = pl.multiple_of(c * TK, TK)
        return (pltpu.make_async_copy(ka_hbm.at[b, pl.ds(k0, TK), :], kabuf.at[slot], sem.at[0, slot]),
                pltpu.make_async_copy(vat_hbm.at[b, :, pl.ds(k0, TK)], vabuf.at[slot], sem.at[1, slot]),
                pltpu.make_async_copy(kb_hbm.at[b, pl.ds(k0, TK), :], kbbuf.at[slot], sem.at[2, slot]),
                pltpu.make_async_copy(vbt_hbm.at[b, :, pl.ds(k0, TK)], vbbuf.at[slot], sem.at[3, slot]))

    def start(c, slot):
        for cp in copies(c, slot):
            cp.start()

    def wait(c, slot):
        for cp in copies(c, slot):
            cp.wait()

    start(0, 0)

    w_rows = kwt_ref[IDX_DIM:IDX_DIM + IDX_HEADS, :] * IDX_W_SCALE
    rr = lax.broadcasted_iota(jnp.int32, (TK, TQ), 0)
    cc = lax.broadcasted_iota(jnp.int32, (TK, TQ), 1)
    admit_diag = rr < (lax.shift_right_arithmetic(cc, int(math.log2(CHUNK))) + 1) * CHUNK

    def score_chunk(c, diagonal):
        k0 = pl.multiple_of(c * TK, TK)
        kblk = kw_ref[pl.ds(k0, TK), :]
        acc = None
        for h in range(IDX_HEADS):
            d = lax.dot_general(kblk, qi_ref[:, h * LANES:(h + 1) * LANES], _NT, preferred_element_type=_F32)
            term = jnp.maximum(d, 0.0) * w_rows[h:h + 1, :]
            acc = term if acc is None else acc + term
        if diagonal:
            acc = jnp.where(admit_diag, acc, -jnp.inf)
        s_ref[pl.ds(k0, TK), :] = _order_key(acc)

    def far_scores(c, carry):
        score_chunk(c, False)
        return carry

    lax.fori_loop(0, i, far_scores, 0)
    score_chunk(i, True)

    def count_ge(cand):
        def body(c, acc):
            k0 = pl.multiple_of(c * TK, TK)
            hit = jnp.where(s_ref[pl.ds(k0, TK), :] >= cand, 1, 0)
            return acc + jnp.sum(hit.reshape(TK // 8, 8, TQ), axis=0)

        part = lax.fori_loop(0, i + 1, body, jnp.zeros((8, TQ), jnp.int32))
        return jnp.sum(part, axis=0, keepdims=True)

    t_row = _select_threshold(count_ge, (1, TQ), n_sel)

    m_ref[...] = jnp.full(m_ref.shape, NEG, _F32)
    l_ref[...] = jnp.zeros_like(l_ref)
    acc_ref[...] = jnp.zeros_like(acc_ref)

    def stream(idx, k_sl, q_sl, v_sl, bias, sel):
        s = lax.dot_general(k_sl, q_sl, _NT, preferred_element_type=_F32)
        if bias is not None:
            s = s + bias
        if sel is not None:
            s = jnp.where(sel, s, NEG)
        m_prev = m_ref[idx:idx + 1, :]
        m_new = jnp.maximum(m_prev, jnp.max(s, axis=0, keepdims=True))
        p = jnp.exp(s - m_new)
        alpha = jnp.exp(m_prev - m_new)
        l_ref[idx:idx + 1, :] = alpha * l_ref[idx:idx + 1, :] + jnp.sum(p, axis=0, keepdims=True)
        acc_ref[idx] = alpha * acc_ref[idx] + jnp.dot(v_sl, p.astype(_BF16), preferred_element_type=_F32)
        m_ref[idx:idx + 1, :] = m_new

    def process(c, slot, bias_half):
        k0 = pl.multiple_of(c * TK, TK)
        sel = s_ref[pl.ds(k0, TK), :] >= t_row
        for h in range(A_HEADS):
            hs = slice(h * LANES, (h + 1) * LANES)
            bias = None if bias_half is None else bias_ref[h, bias_half * TK:(bias_half + 1) * TK, :]
            stream(h, kabuf[slot, :, hs], qa_ref[:, hs], vabuf[slot, hs, :], bias, sel)
        for h in range(B_HEADS):
            hs = slice(h * LANES, (h + 1) * LANES)
            bias = None if bias_half is None else bias_ref[A_HEADS + h, bias_half * TK:(bias_half + 1) * TK, :]
            for half, q_ref in enumerate((qb1_ref, qb2_ref)):
                stream(A_HEADS + 2 * h + half, kbbuf[slot, :, hs], q_ref[:, hs], vbbuf[slot, hs, :], bias, None)

    def far_body(c, carry):
        slot = lax.rem(c, 2)
        wait(c, slot)
        start(c + 1, 1 - slot)
        process(c, slot, None)
        return carry

    lax.fori_loop(0, jnp.maximum(i - 1, 0), far_body, 0)

    @pl.when(i >= 1)
    def _():
        c = i - 1
        slot = lax.rem(c, 2)
        wait(c, slot)
        start(i, 1 - slot)
        process(c, slot, 0)

    slot = lax.rem(i, 2)
    wait(i, slot)
    process(i, slot, 1)

    for h in range(A_HEADS):
        o_t = acc_ref[h] / l_ref[h:h + 1, :]
        o_ref[:, h * LANES:(h + 1) * LANES] = o_t.T.astype(o_ref.dtype)
    lam = _lambda(lam_ref, lam_init)
    for h in range(B_HEADS):
        i1 = A_HEADS + 2 * h
        d = acc_ref[i1] / l_ref[i1:i1 + 1, :] - lam * (acc_ref[i1 + 1] / l_ref[i1 + 1:i1 + 2, :])
        y = d * lax.rsqrt(jnp.mean(d * d, axis=0, keepdims=True) + SUBLN_EPS) * gsub_ref[...]
        y = y * (1.0 - lam_init)
        off = A_HEADS * A_HEAD_DIM + h * B_V_DIM
        o_ref[:, off:off + B_V_DIM] = y.T.astype(o_ref.dtype)
    del n_streams


def _attn_prompt(qa, qi, qb1, qb2, kwt, kwbf, bias, lam_vecs, g_subln, kabf, vat, kbbf, vbt, *, lam_init):
    bsz, t, _ = qa.shape
    n_sel = min(TOPK_MAX, t // 4)
    n_streams = A_HEADS + 2 * B_HEADS

    def qspec(width):
        return pl.BlockSpec((None, TQ, width), lambda b, i: (b, i, 0))

    hbm = pl.BlockSpec(memory_space=pl.ANY)
    return pl.pallas_call(
        functools.partial(_attn_prompt_kernel, n_sel=n_sel, lam_init=lam_init),
        grid=(bsz, t // TQ),
        in_specs=[qspec(512), qspec(1024), qspec(512), qspec(512),
                  pl.BlockSpec((None, LANES, TQ), lambda b, i: (b, 0, i)),
                  pl.BlockSpec((None, t, LANES), lambda b, i: (b, 0, 0)),
                  _resident(bias.shape, lambda b, i: (0, 0, 0)),
                  pl.BlockSpec(lam_vecs.shape, lambda b, i: (0, 0)),
                  pl.BlockSpec((B_V_DIM, 1), lambda b, i: (0, 0)),
                  hbm, hbm, hbm, hbm],
        out_specs=pl.BlockSpec((None, TQ, D_MIX), lambda b, i: (b, i, 0)),
        out_shape=jax.ShapeDtypeStruct((bsz, t, D_MIX), _BF16),
        scratch_shapes=[pltpu.VMEM((t, TQ), jnp.int32),
                        pltpu.VMEM((2, TK, 512), _BF16), pltpu.VMEM((2, 512, TK), _BF16),
                        pltpu.VMEM((2, TK, 512), _BF16), pltpu.VMEM((2, 512, TK), _BF16),
                        pltpu.SemaphoreType.DMA((4, 2)),
                        pltpu.VMEM((n_streams, B_V_DIM, TQ), _F32),
                        pltpu.VMEM((16, TQ), _F32), pltpu.VMEM((16, TQ), _F32)],
        compiler_params=_cparams(("arbitrary", "arbitrary")),
        name="attn_prompt",
    )(qa, qi, qb1, qb2, kwt, kwbf, bias, lam_vecs, g_subln.reshape(B_V_DIM, 1), kabf, vat, kbbf, vbt)


def _attn_sample_kernel(qa_ref, qi_ref, qb1_ref, qb2_ref, kwn_ref, kan_ref, van_ref, kbn_ref, vbn_ref,
                        bias_p_ref, bias_n_ref, lam_ref, gsub_ref,
                        cidx_ref, cka_ref, cva_ref, ckb_ref, cvb_ref, o_ref,
                        sp_ref, sn_ref, t_ref, acc_ref, m_ref, l_ref,
                        *, n_sel, lam_init, n_new, nk):
    k = pl.program_id(1)
    tk = cka_ref.shape[0]

    def idx_scores(kidx_bf):
        acc = None
        for h in range(IDX_HEADS):
            q_h = qi_ref[:, h * LANES:h * LANES + IDX_DIM]
            d = lax.dot_general(q_h, kidx_bf, _NT, preferred_element_type=_F32)
            w_h = kwn_ref[0:n_new, IDX_DIM + h:IDX_DIM + h + 1] * IDX_W_SCALE
            term = jnp.maximum(d, 0.0) * w_h
            acc = term if acc is None else acc + term
        return acc

    @pl.when(k == 0)
    def _():
        for j in range(nk):
            kidx = cidx_ref[j * tk:(j + 1) * tk, :].astype(_BF16)
            sp_ref[j] = _order_key(idx_scores(kidx))
        s_new = idx_scores(kwn_ref[:, 0:IDX_DIM].astype(_BF16))
        col = lax.broadcasted_iota(jnp.int32, s_new.shape, 1)
        sn_ref[...] = _order_key(jnp.where(col < n_new, s_new, -jnp.inf))

        def count_ge(cand):
            tot = jnp.sum(jnp.where(sn_ref[...] >= cand, 1, 0), axis=1, keepdims=True)
            for j in range(nk):
                tot = tot + jnp.sum(jnp.where(sp_ref[j] >= cand, 1, 0), axis=1, keepdims=True)
            return tot

        t_ref[...] = _select_threshold(count_ge, (n_new, 1), n_sel)
        m_ref[...] = jnp.full(m_ref.shape, NEG, _F32)
        l_ref[...] = jnp.zeros_like(l_ref)
        acc_ref[...] = jnp.zeros_like(acc_ref)

    def stream(idx, q_sl, k_sl, v_sl, bias, sel):
        s = lax.dot_general(q_sl, k_sl, _NT, preferred_element_type=_F32)
        if bias is not None:
            s = s + bias
        if sel is not None:
            s = jnp.where(sel, s, NEG)
        m_prev = m_ref[idx]
        m_new = jnp.maximum(m_prev, jnp.max(s, axis=1, keepdims=True))
        p = jnp.exp(s - m_new)
        alpha = jnp.exp(m_prev - m_new)
        l_ref[idx] = alpha * l_ref[idx] + jnp.sum(p, axis=1, keepdims=True)
        acc_ref[idx] = alpha * acc_ref[idx] + jnp.dot(p.astype(_BF16), v_sl, preferred_element_type=_F32)
        m_ref[idx] = m_new

    def process(ka, va, kb, vb, keys, bias_ref):
        sel = keys >= t_ref[...]
        for h in range(A_HEADS):
            hs = slice(h * LANES, (h + 1) * LANES)
            bias = None if bias_ref is None else bias_ref[h]
            stream(h, qa_ref[:, hs], ka[:, hs], va[:, hs], bias, sel)
        for h in range(B_HEADS):
            hs = slice(h * LANES, (h + 1) * LANES)
            bias = None if bias_ref is None else bias_ref[A_HEADS + h]
            for half, q_ref in enumerate((qb1_ref, qb2_ref)):
                stream(A_HEADS + 2 * h + half, q_ref[:, hs], kb[:, hs], vb[:, hs], bias, None)

    def cached():
        return (cka_ref[...].astype(_BF16), cva_ref[...].astype(_BF16),
                ckb_ref[...].astype(_BF16), cvb_ref[...].astype(_BF16), sp_ref[k])

    @pl.when(k < nk - 1)
    def _():
        process(*cached(), None)

    @pl.when(k == nk - 1)
    def _():
        process(*cached(), bias_p_ref)
        process(kan_ref[...], van_ref[...], kbn_ref[...], vbn_ref[...], sn_ref[...], bias_n_ref)
        for h in range(A_HEADS):
            o_ref[:, h * LANES:(h + 1) * LANES] = (acc_ref[h] / l_ref[h]).astype(o_ref.dtype)
        lam = _lambda(lam_ref, lam_init)
        for h in range(B_HEADS):
            i1 = A_HEADS + 2 * h
            d = acc_ref[i1] / l_ref[i1] - lam * (acc_ref[i1 + 1] / l_ref[i1 + 1])
            y = d * lax.rsqrt(jnp.mean(d * d, axis=1, keepdims=True) + SUBLN_EPS) * gsub_ref[...]
            y = y * (1.0 - lam_init)
            off = A_HEADS * A_HEAD_DIM + h * B_V_DIM
            o_ref[:, off:off + B_V_DIM] = y.astype(o_ref.dtype)


def _attn_sample(qa, qi, qb1, qb2, kw_new, ka_new, va_new, kb_new, vb_new, bias_p, bias_n, lam_vecs, g_subln,
                 c_idx, c_ka, c_va, c_kb, c_vb, *, lam_init):
    bsz, n_new, _ = qa.shape
    past = c_ka.shape[1]
    tk = min(SAMPLE_TK, past)
    nk = past // tk
    n_sel = min(TOPK_MAX, (past + n_new) // 4)
    n_streams = A_HEADS + 2 * B_HEADS

    def per_b(rows, width):
        return pl.BlockSpec((None, rows, width), lambda b, k: (b, 0, 0))

    def cache(width):
        return pl.BlockSpec((None, tk, width), lambda b, k: (b, k, 0))

    def const(shape):
        return pl.BlockSpec(shape, lambda b, k: (0,) * len(shape))

    return pl.pallas_call(
        functools.partial(_attn_sample_kernel, n_sel=n_sel, lam_init=lam_init, n_new=n_new, nk=nk),
        grid=(bsz, nk),
        in_specs=[per_b(n_new, 512), per_b(n_new, 1024), per_b(n_new, 512), per_b(n_new, 512),
                  per_b(NEW_PAD, LANES), per_b(NEW_PAD, 512), per_b(NEW_PAD, 512), per_b(NEW_PAD, 512),
                  per_b(NEW_PAD, 512),
                  const(bias_p.shape), const(bias_n.shape), const(lam_vecs.shape), const((1, B_V_DIM)),
                  per_b(past, IDX_DIM), cache(512), cache(512), cache(512), cache(512)],
        out_specs=per_b(n_new, D_MIX),
        out_shape=jax.ShapeDtypeStruct((bsz, n_new, D_MIX), _BF16),
        scratch_shapes=[pltpu.VMEM((nk, n_new, tk), jnp.int32), pltpu.VMEM((n_new, NEW_PAD), jnp.int32),
                        pltpu.VMEM((n_new, 1), jnp.int32),
                        pltpu.VMEM((n_streams, n_new, B_V_DIM), _F32),
                        pltpu.VMEM((n_streams, n_new, 1), _F32), pltpu.VMEM((n_streams, n_new, 1), _F32)],
        compiler_params=_cparams(("parallel", "arbitrary")),
        name="attn_sample",
    )(qa, qi, qb1, qb2, kw_new, ka_new, va_new, kb_new, vb_new, bias_p, bias_n, lam_vecs,
      g_subln.reshape(1, B_V_DIM), c_idx, c_ka, c_va, c_kb, c_vb)


def _outproj_kernel(o_ref, x_ref, gt_ref, gpost_ref, w_ref, y_ref):
    mix = jnp.dot(o_ref[...], w_ref[...], preferred_element_type=_F32)
    y_ref[...] = x_ref[...] + gt_ref[...] * _rms(mix, gpost_ref[...], RMS_EPS)


def _outproj(o, x, gate, g_post, w_out_bf, *, per_token):
    bsz, t, d = x.shape
    tm = min(TOKEN_TILE, t)
    tok = pl.BlockSpec((None, tm, d), lambda b, i: (b, i, 0))
    return pl.pallas_call(
        _outproj_kernel,
        grid=(bsz, t // tm),
        in_specs=[pl.BlockSpec((None, tm, D_MIX), lambda b, i: (b, i, 0)), tok, _mod_specs(per_token, tm, d),
                  pl.BlockSpec((1, d), lambda b, i: (0, 0)),
                  _resident(w_out_bf.shape, lambda b, i: (0, 0))],
        out_specs=tok,
        out_shape=jax.ShapeDtypeStruct(x.shape, _F32),
        compiler_params=_cparams(("parallel", "parallel")),
        name="outproj",
    )(o, x, gate, g_post.reshape(1, d), w_out_bf)


def _pack_w_in(w_in):
    d = w_in.shape[0]
    parts, off = [], 0
    for n in IN_SIZES:
        parts.append(w_in[:, off:off + n])
        off += n
    q_a, k_a, v_a, q_idx, k_idx, w_idx, q_b, k_b, v_b = parts
    zi = jnp.zeros((d, IDX_HEADS, LANES - IDX_DIM), w_in.dtype)
    qi = jnp.concatenate([q_idx.reshape(d, IDX_HEADS, IDX_DIM), zi], axis=-1).reshape(d, IDX_HEADS * LANES)
    qb = q_b.reshape(d, B_HEADS, 2, B_QK_DIM)
    zb = jnp.zeros((d, B_HEADS, B_QK_DIM), w_in.dtype)
    qb1 = jnp.concatenate([qb[:, :, 0], zb], axis=-1).reshape(d, B_HEADS * LANES)
    qb2 = jnp.concatenate([zb, qb[:, :, 1]], axis=-1).reshape(d, B_HEADS * LANES)
    kw = jnp.concatenate([k_idx, w_idx, jnp.zeros((d, LANES - IDX_DIM - IDX_HEADS), w_in.dtype)], axis=-1)
    packed = jnp.concatenate([q_a, qi, qb1, qb2, k_a, v_a, kw, k_b, v_b], axis=-1)
    assert packed.shape[1] == N_PACKED
    return packed.astype(_BF16)


def _pack_ffn(w_gate, w_up, w_down):
    d, f = w_gate.shape
    n = f // FF_CHUNK
    wg3 = w_gate.reshape(d, n, FF_CHUNK).transpose(1, 0, 2).astype(_BF16)
    wu3 = w_up.reshape(d, n, FF_CHUNK).transpose(1, 0, 2).astype(_BF16)
    wd3 = w_down.reshape(n, FF_CHUNK, d).astype(_BF16)
    return wg3, wu3, wd3


def _mods(mod_rows, per_token_len):
    rows = mod_rows.shape[0]
    m = mod_rows.reshape(rows, N_SUB, 3, 1, D_MODEL)
    out = []
    for s in range(N_SUB):
        trip = []
        for kind in range(3):
            v = m[:, s, kind]
            if per_token_len:
                v = jnp.broadcast_to(v, (rows, per_token_len, D_MODEL)).reshape(1, rows * per_token_len, D_MODEL)
            trip.append(v)
        out.append(tuple(trip))
    return out


def _split_heads(x, b, t):
    return x.reshape(1, b, t, -1, LANES)


def kernel(x_prompt, x_sample, cache_a_k, cache_a_v, cache_a_kidx, cache_b_k, cache_b_v, c_prompt, c_sample, rel_bias, w_ada, b_ada, g_pre, g_post, ffn1_w_gate, ffn1_w_up, ffn1_w_down, ffn2_w_gate, ffn2_w_up, ffn2_w_down, w_in, w_out, lambda_q1, lambda_k1, lambda_q2, lambda_k2, g_subln):
    depth = w_in.shape[0]
    assert depth == 1, "stacked caches are returned per layer; only one layer is staged"
    layer = 0
    bp, tp, d = x_prompt.shape
    bs, ts, _ = x_sample.shape
    past = cache_a_k.shape[2]
    assert d == D_MODEL and tp % TQ == 0 and ts <= 16 and past % min(SAMPLE_TK, past) == 0
    lam_init = 0.8 - 0.6 * math.exp(-0.3 * layer)

    n_c = bp + bs
    rows = -(-n_c // 8) * 8
    c_all = jnp.concatenate([c_prompt, c_sample, jnp.zeros((rows - n_c, d), _F32)], axis=0)
    mod = _ada(c_all, w_ada[layer], b_ada[layer])
    mods_p = _mods(mod[:bp], 0)
    mods_s = _mods(mod[bp:bp + bs], ts)

    ffn1 = _pack_ffn(ffn1_w_gate[layer], ffn1_w_up[layer], ffn1_w_down[layer])
    ffn2 = _pack_ffn(ffn2_w_gate[layer], ffn2_w_up[layer], ffn2_w_down[layer])
    w_packed = _pack_w_in(w_in[layer])
    w_out_bf = w_out[layer].astype(_BF16)
    gp, gq = g_pre[layer], g_post[layer]
    lam_vecs = jnp.stack([lambda_q1[layer], lambda_k1[layer], lambda_q2[layer], lambda_k2[layer]], axis=0)
    rel_bias_t = rel_bias.T
    gsub = g_subln[layer]

    (sh, sc, gt) = mods_p[0]
    x1 = _ffn(x_prompt, sh, sc, gt, gp[0], gq[0], *ffn1, per_token=False, res_w=FFN_RES)
    (sh, sc, gt) = mods_p[1]
    (qa, qi, qb1, qb2, ka, va, kw, kb, vb, kabf, vat, kwbf, kwt, kbbf, vbt) = _inproj(
        x1, sh, sc, gp[1], w_packed, per_token=False, transposed=True)
    bias_p = _bias_tiles(rel_bias_t, 2 * TK, TQ, q_on_cols=True, q0=TK, k0=0, n_keys=2 * TK)
    o = _attn_prompt(qa, qi, qb1, qb2, kwt, kwbf, bias_p, lam_vecs, gsub, kabf, vat, kbbf, vbt, lam_init=lam_init)
    x2 = _outproj(o, x1, gt, gq[1], w_out_bf, per_token=False)
    (sh, sc, gt) = mods_p[2]
    y_prompt = _ffn(x2, sh, sc, gt, gp[2], gq[2], *ffn2, per_token=False, res_w=FFN_RES)
    rows_p = (_split_heads(ka, bp, tp), _split_heads(va, bp, tp), kw[..., :IDX_DIM].reshape(1, bp, tp, IDX_DIM),
              _split_heads(kb, bp, tp), _split_heads(vb, bp, tp))

    xs = x_sample.reshape(1, bs * ts, d)
    (sh, sc, gt) = mods_s[0]
    xs1 = _ffn(xs, sh, sc, gt, gp[0], gq[0], *ffn1, per_token=True, res_w=FFN_RES)
    (sh, sc, gt) = mods_s[1]
    (sqa, sqi, sqb1, sqb2, ska, sva, skw, skb, svb, skabf, svabf, skbbf, svbbf) = _inproj(
        xs1, sh, sc, gp[1], w_packed, per_token=True, transposed=False)

    def per_seq(a):
        return a.reshape(bs, ts, a.shape[-1])

    def pad_new(a):
        return jnp.pad(per_seq(a), ((0, 0), (0, NEW_PAD - ts), (0, 0)))

    tk_s = min(SAMPLE_TK, past)
    n_keys = past + ts
    bias_sp = _bias_tiles(rel_bias_t, ts, tk_s, q_on_cols=False, q0=past, k0=past - tk_s, n_keys=n_keys)
    bias_sn = _bias_tiles(rel_bias_t, ts, NEW_PAD, q_on_cols=False, q0=past, k0=past, n_keys=n_keys)
    o_s = _attn_sample(per_seq(sqa), per_seq(sqi), per_seq(sqb1), per_seq(sqb2),
                       pad_new(skw), pad_new(skabf), pad_new(svabf), pad_new(skbbf), pad_new(svbbf),
                       bias_sp, bias_sn, lam_vecs, gsub,
                       cache_a_kidx[layer], cache_a_k[layer].reshape(bs, past, -1),
                       cache_a_v[layer].reshape(bs, past, -1), cache_b_k[layer].reshape(bs, past, -1),
                       cache_b_v[layer].reshape(bs, past, -1), lam_init=lam_init)
    xs2 = _outproj(o_s.reshape(1, bs * ts, D_MIX), xs1, gt, gq[1], w_out_bf, per_token=True)
    (sh, sc, gt) = mods_s[2]
    y_sample = _ffn(xs2, sh, sc, gt, gp[2], gq[2], *ffn2, per_token=True, res_w=FFN_RES).reshape(bs, ts, d)
    rows_s = (_split_heads(ska, bs, ts), _split_heads(sva, bs, ts), skw[..., :IDX_DIM].reshape(1, bs, ts, IDX_DIM),
              _split_heads(skb, bs, ts), _split_heads(svb, bs, ts))

    return (y_prompt, y_sample) + rows_p + rows_s
```

```python
import functools
import math

import jax
import jax.numpy as jnp
from jax import lax
from jax.experimental import pallas as pl
from jax.experimental.pallas import tpu as pltpu

D_MODEL = 1024
CHUNK = 64
A_HEADS = 4
A_HEAD_DIM = 128
IDX_HEADS = 8
IDX_DIM = 64
TOPK_MAX = 256
B_HEADS = 4
B_QK_DIM = 64
B_V_DIM = 128
D_MIX = A_HEADS * A_HEAD_DIM + B_HEADS * B_V_DIM
D_FF = 2816
FFN_RES = 0.5
N_BUCKETS = 32
MAX_DISTANCE = 128
N_SUB = 3
RMS_EPS = 1e-6
SUBLN_EPS = 1e-5
IN_SIZES = (512, 512, 512, 512, 64, 8, 512, 512, 512)

LANES = 128
VMEM_LIMIT_BYTES = 56 * 1024 * 1024

FF_CHUNK = 256
TOKEN_TILE = 512
TQ = 256
TK = 256
SAMPLE_TK = 1024
NEW_PAD = 128
QK_LOOKAHEAD = 12

NEG = -1e30
LOG2E = math.log2(math.e)
A_SCALE = A_HEAD_DIM ** -0.5 * LOG2E
B_SCALE = B_QK_DIM ** -0.5 * LOG2E
ONES_ROWS = 16
IDX_SCALE = IDX_DIM ** -0.5
IDX_W_SCALE = IDX_HEADS ** -0.5
INT_MIN = -(2 ** 31)

SEG_QA, SEG_QI, SEG_QB1, SEG_QB2, SEG_KA, SEG_VA, SEG_KW, SEG_KB, SEG_VB = (
    (0, 512), (512, 1024), (1536, 512), (2048, 512), (2560, 512), (3072, 512), (3584, 128), (3712, 512), (4224, 512))
N_PACKED = 4736

_NT = (((1,), (1,)), ((), ()))
_F32 = jnp.float32
_BF16 = jnp.bfloat16


def _cparams(sem):
    return pltpu.CompilerParams(dimension_semantics=sem, vmem_limit_bytes=VMEM_LIMIT_BYTES)


def _resident(shape, index_map):
    return pl.BlockSpec(shape, index_map, pipeline_mode=pl.Buffered(1))


def _ada_kernel(c_ref, w_ref, b_ref, o_ref):
    c = c_ref[...]
    s = c * jax.nn.sigmoid(c)
    o_ref[...] = jnp.dot(s, w_ref[...], precision=lax.Precision.HIGHEST,
                         preferred_element_type=_F32) + b_ref[...]


def _ada(c, w_ada, b_ada):
    rows, d = c.shape
    n = w_ada.shape[1]
    tn = 1024
    return pl.pallas_call(
        _ada_kernel,
        grid=(n // tn,),
        in_specs=[pl.BlockSpec((rows, d), lambda j: (0, 0)),
                  pl.BlockSpec((d, tn), lambda j: (0, j)),
                  pl.BlockSpec((1, tn), lambda j: (0, j))],
        out_specs=pl.BlockSpec((rows, tn), lambda j: (0, j)),
        out_shape=jax.ShapeDtypeStruct((rows, n), _F32),
        compiler_params=_cparams(("parallel",)),
        name="ada_mod",
    )(c, w_ada, b_ada.reshape(1, n))


def _rms(x, g, eps):
    return x * lax.rsqrt(jnp.mean(x * x, axis=-1, keepdims=True) + eps) * g


def _mod_specs(per_token, tm, d):
    if per_token:
        return pl.BlockSpec((None, tm, d), lambda b, t: (b, t, 0))
    return pl.BlockSpec((None, 1, d), lambda b, t: (b, 0, 0))


def _ffn_kernel(x_ref, sh_ref, sc_ref, gt_ref, gpre_ref, gpost_ref, wg_ref, wu_ref, wd_ref,
                o_ref, h_ref, acc_ref, *, n_chunks, res_w):
    x = x_ref[...]
    h = _rms(x, gpre_ref[...], RMS_EPS) * (1.0 + sc_ref[...]) + sh_ref[...]
    h_ref[...] = h.astype(_BF16)
    acc_ref[...] = jnp.zeros_like(acc_ref)

    def body(c, carry):
        hb = h_ref[...]
        g = jnp.dot(hb, wg_ref[c], preferred_element_type=_F32)
        u = jnp.dot(hb, wu_ref[c], preferred_element_type=_F32)
        a = (g * jax.nn.sigmoid(g)) * u
        acc_ref[...] += jnp.dot(a.astype(_BF16), wd_ref[c], preferred_element_type=_F32)
        return carry

    lax.fori_loop(0, n_chunks, body, 0)
    y = _rms(acc_ref[...], gpost_ref[...], RMS_EPS)
    o_ref[...] = x + res_w * gt_ref[...] * y


def _ffn(x, shift, scale, gate, g_pre, g_post, wg3, wu3, wd3, *, per_token, res_w):
    bsz, t, d = x.shape
    tm = min(TOKEN_TILE, t)
    n_chunks = wg3.shape[0]
    tok = pl.BlockSpec((None, tm, d), lambda b, i: (b, i, 0))
    vec = pl.BlockSpec((1, d), lambda b, i: (0, 0))
    mod = _mod_specs(per_token, tm, d)
    return pl.pallas_call(
        functools.partial(_ffn_kernel, n_chunks=n_chunks, res_w=res_w),
        grid=(bsz, t // tm),
        in_specs=[tok, mod, mod, mod, vec, vec,
                  _resident(wg3.shape, lambda b, i: (0, 0, 0)),
                  _resident(wu3.shape, lambda b, i: (0, 0, 0)),
                  _resident(wd3.shape, lambda b, i: (0, 0, 0))],
        out_specs=tok,
        out_shape=jax.ShapeDtypeStruct(x.shape, _F32),
        scratch_shapes=[pltpu.VMEM((tm, d), _BF16), pltpu.VMEM((tm, d), _F32)],
        compiler_params=_cparams(("parallel", "parallel")),
        name="ffn",
    )(x, shift, scale, gate, g_pre.reshape(1, d), g_post.reshape(1, d), wg3, wu3, wd3)


def _inproj_kernel(x_ref, sh_ref, sc_ref, gpre_ref, w_ref, *out_refs, transposed):
    if transposed:
        (qa_ref, qi_ref, qb1_ref, qb2_ref, ka_ref, va_ref, kw_ref, kb_ref, vb_ref,
         kabf_ref, vat_ref, kwbf_ref, kwt_ref, kbbf_ref, vbt_ref) = out_refs
    else:
        (qa_ref, qi_ref, qb1_ref, qb2_ref, ka_ref, va_ref, kw_ref, kb_ref, vb_ref,
         kabf_ref, vabf_ref, kbbf_ref, vbbf_ref) = out_refs
    h = _rms(x_ref[...], gpre_ref[...], RMS_EPS) * (1.0 + sc_ref[...]) + sh_ref[...]
    hb = h.astype(_BF16)

    def proj(seg):
        off, width = seg
        return jnp.dot(hb, w_ref[:, off:off + width], preferred_element_type=_F32)

    qa_ref[...] = (proj(SEG_QA) * A_SCALE).astype(_BF16)
    qi_ref[...] = (proj(SEG_QI) * IDX_SCALE).astype(_BF16)
    qb1_ref[...] = (proj(SEG_QB1) * B_SCALE).astype(_BF16)
    qb2_ref[...] = (proj(SEG_QB2) * B_SCALE).astype(_BF16)
    tm = x_ref.shape[0]

    def store_rows(ref, val, heads):
        for hd in range(heads):
            ref[pl.ds(hd, tm, stride=heads), :] = val[:, hd * LANES:(hd + 1) * LANES]

    ka = proj(SEG_KA)
    store_rows(ka_ref, ka, A_HEADS)
    kabf_ref[...] = ka.astype(_BF16)
    kb = proj(SEG_KB)
    store_rows(kb_ref, kb, B_HEADS)
    kbbf_ref[...] = kb.astype(_BF16)
    kw = proj(SEG_KW)
    kw_ref[...] = kw
    va = proj(SEG_VA)
    store_rows(va_ref, va, A_HEADS)
    vb = proj(SEG_VB)
    store_rows(vb_ref, vb, B_HEADS)
    if transposed:
        kwbf_ref[...] = kw.astype(_BF16)
        kwt_ref[...] = kw.T
        vat_ref[...] = va.T.astype(_BF16)
        vbt_ref[...] = vb.T.astype(_BF16)
    else:
        vabf_ref[...] = va.astype(_BF16)
        vbbf_ref[...] = vb.astype(_BF16)


def _inproj(x, shift, scale, g_pre, w_packed, *, per_token, transposed):
    bsz, t, d = x.shape
    tm = min(TOKEN_TILE, t)

    def tok(width, dtype):
        return (pl.BlockSpec((None, tm, width), lambda b, i: (b, i, 0)),
                jax.ShapeDtypeStruct((bsz, t, width), dtype))

    def tr(width, dtype):
        return (pl.BlockSpec((None, width, tm), lambda b, i: (b, 0, i)),
                jax.ShapeDtypeStruct((bsz, width, t), dtype))

    def rows(heads):
        return (pl.BlockSpec((None, tm * heads, LANES), lambda b, i: (b, i, 0)),
                jax.ShapeDtypeStruct((bsz, t * heads, LANES), _F32))

    outs = [tok(512, _BF16), tok(1024, _BF16), tok(512, _BF16), tok(512, _BF16),
            rows(A_HEADS), rows(A_HEADS), tok(LANES, _F32), rows(B_HEADS), rows(B_HEADS)]
    if transposed:
        outs += [tok(512, _BF16), tr(512, _BF16), tok(LANES, _BF16), tr(LANES, _F32),
                 tok(512, _BF16), tr(512, _BF16)]
    else:
        outs += [tok(512, _BF16), tok(512, _BF16), tok(512, _BF16), tok(512, _BF16)]
    mod = _mod_specs(per_token, tm, d)
    return pl.pallas_call(
        functools.partial(_inproj_kernel, transposed=transposed),
        grid=(bsz, t // tm),
        in_specs=[pl.BlockSpec((None, tm, d), lambda b, i: (b, i, 0)), mod, mod,
                  pl.BlockSpec((1, d), lambda b, i: (0, 0)),
                  _resident(w_packed.shape, lambda b, i: (0, 0))],
        out_specs=[o[0] for o in outs],
        out_shape=[o[1] for o in outs],
        compiler_params=_cparams(("parallel", "parallel")),
        name="inproj",
    )(x, shift, scale, g_pre.reshape(1, d), w_packed)


def _bucket_thresholds():
    nb = N_BUCKETS // 2
    max_exact = nb // 2
    ratio = MAX_DISTANCE / max_exact
    out = []
    for j in range(1, nb - max_exact):
        out.append(int(math.ceil(max_exact * ratio ** (j / (nb - max_exact)) - 1e-9)))
    return nb, max_exact, tuple(out)


def _bias_tile_kernel(tab_ref, o_ref, *, q_on_cols, q0, k0, n_keys):
    hh = pl.program_id(0)
    shape = o_ref.shape
    r = lax.broadcasted_iota(jnp.int32, shape, 0)
    c = lax.broadcasted_iota(jnp.int32, shape, 1)
    qpos = q0 + (c if q_on_cols else r)
    kpos = k0 + (r if q_on_cols else c)
    rel = kpos - qpos
    n = jnp.abs(rel)
    nb, max_exact, thresholds = _bucket_thresholds()
    large = jnp.full(shape, max_exact, jnp.int32)
    for th in thresholds:
        large = large + jnp.where(n >= th, 1, 0)
    bucket = jnp.where(rel > 0, nb, 0) + jnp.where(n < max_exact, n, large)
    val = jnp.zeros(shape, _F32)
    for bkt in range(N_BUCKETS):
        val = jnp.where(bucket == bkt, tab_ref[hh, bkt], val)
    val = (val - tab_ref[hh, nb - 1]) * LOG2E
    chunk_end = (lax.shift_right_arithmetic(qpos, int(math.log2(CHUNK))) + 1) * CHUNK
    admit = (kpos < chunk_end) & (kpos >= 0) & (kpos < n_keys)
    o_ref[...] = jnp.where(admit, val, NEG)


def _bias_tiles(rel_bias_t, rows, cols, *, q_on_cols, q0, k0, n_keys):
    n_heads = rel_bias_t.shape[0]
    return pl.pallas_call(
        functools.partial(_bias_tile_kernel, q_on_cols=q_on_cols, q0=q0, k0=k0, n_keys=n_keys),
        grid=(n_heads,),
        in_specs=[pl.BlockSpec(memory_space=pltpu.SMEM)],
        out_specs=pl.BlockSpec((None, rows, cols), lambda h: (h, 0, 0)),
        out_shape=jax.ShapeDtypeStruct((n_heads, rows, cols), _F32),
        compiler_params=_cparams(("parallel",)),
        name="bias_tiles",
    )(rel_bias_t)


def _order_key(score):
    bits = lax.bitcast_convert_type(score, jnp.int32)
    return bits ^ (lax.shift_right_arithmetic(bits, 31) & 0x7FFFFFFF)


def _select_threshold(count_ge, shape, n_sel, n_total):
    def cond(carry):
        it, _, _, worst = carry
        return (it < 32) & (worst > n_sel)

    def body(carry):
        it, t_u, c_t, _ = carry
        worst = jnp.max(c_t)
        bit = lax.shift_left(jnp.int32(1), 31 - it)
        cand_u = t_u | bit
        cnt = count_ge(cand_u ^ INT_MIN)
        take = cnt >= n_sel
        t_u = jnp.where(take, cand_u, t_u)
        c_t = jnp.where(take, cnt, c_t)
        return it + 1, t_u, c_t, worst

    n_total = jnp.asarray(n_total, jnp.int32)
    init = (jnp.int32(0), jnp.zeros(shape, jnp.int32), jnp.full(shape, n_total, jnp.int32), n_total)
    _, t_u, _, _ = lax.while_loop(cond, body, init)
    return t_u ^ INT_MIN


def _lambda(lam_ref, lam_init):
    l1 = jnp.sum(lam_ref[0:1, :] * lam_ref[1:2, :], axis=-1, keepdims=True)
    l2 = jnp.sum(lam_ref[2:3, :] * lam_ref[3:4, :], axis=-1, keepdims=True)
    return jnp.exp(l1) - jnp.exp(l2) + lam_init


def _attn_prompt_kernel(qa_ref, qi_ref, qb1_ref, qb2_ref, kwt_ref, kw_ref, bias_ref, lam_ref, gsub_ref,
                        ka_hbm, vat_hbm, kb_hbm, vbt_hbm, o_ref,
                        s_ref, kabuf, vabuf, kbbuf, vbbuf, sem, acc_ref, m_ref, l_ref,
                        *, n_sel, lam_init):
    b = pl.program_id(0)
    i = pl.program_id(1)

    def copies(c, slot):
        k0 = pl.multiple_of(c * TK, TK)
        vrows = pl.ds(0, B_V_DIM)
        return (pltpu.make_async_copy(ka_hbm.at[b, pl.ds(k0, TK), :], kabuf.at[slot], sem.at[0, slot]),
                pltpu.make_async_copy(vat_hbm.at[b, :, :, pl.ds(k0, TK)], vabuf.at[slot, :, vrows, :], sem.at[1, slot]),
                pltpu.make_async_copy(kb_hbm.at[b, pl.ds(k0, TK), :], kbbuf.at[slot], sem.at[2, slot]),
                pltpu.make_async_copy(vbt_hbm.at[b, :, :, pl.ds(k0, TK)], vbbuf.at[slot, :, vrows, :], sem.at[3, slot]))

    def start(c, slot):
        for cp in copies(c, slot):
            cp.start()

    def wait(c, slot):
        for cp in copies(c, slot):
            cp.wait()

    start(0, 0)
    ones = jnp.ones((2, A_HEADS, ONES_ROWS, TK), _BF16)
    vabuf[:, :, B_V_DIM:, :] = ones
    vbbuf[:, :, B_V_DIM:, :] = ones

    w_rows = kwt_ref[IDX_DIM:IDX_DIM + IDX_HEADS, :] * IDX_W_SCALE
    rr = lax.broadcasted_iota(jnp.int32, (TK, TQ), 0)
    cc = lax.broadcasted_iota(jnp.int32, (TK, TQ), 1)
    admit_diag = rr < (lax.shift_right_arithmetic(cc, int(math.log2(CHUNK))) + 1) * CHUNK

    def score_chunk(c, diagonal):
        k0 = pl.multiple_of(c * TK, TK)
        kblk = kw_ref[pl.ds(k0, TK), :]
        acc = None
        for h in range(IDX_HEADS):
            d = lax.dot_general(kblk, qi_ref[:, h * LANES:(h + 1) * LANES], _NT, preferred_element_type=_F32)
            term = jnp.maximum(d, 0.0) * w_rows[h:h + 1, :]
            acc = term if acc is None else acc + term
        if diagonal:
            acc = jnp.where(admit_diag, acc, -jnp.inf)
        s_ref[pl.ds(k0, TK), :] = _order_key(acc)

    def far_scores(c, carry):
        score_chunk(c, False)
        return carry

    lax.fori_loop(0, i, far_scores, 0)
    score_chunk(i, True)

    def count_ge(cand):
        def body(c, acc):
            k0 = pl.multiple_of(c * TK, TK)
            hit = jnp.where(s_ref[pl.ds(k0, TK), :] >= cand, 1, 0)
            return acc + jnp.sum(hit.reshape(TK // 8, 8, TQ), axis=0)

        part = lax.fori_loop(0, i + 1, body, jnp.zeros((8, TQ), jnp.int32))
        return jnp.sum(part, axis=0, keepdims=True)

    t_row = _select_threshold(count_ge, (1, TQ), n_sel, (i + 1) * TK)

    m_ref[...] = jnp.full(m_ref.shape, NEG, _F32)
    l_ref[...] = jnp.zeros_like(l_ref)
    acc_ref[...] = jnp.zeros_like(acc_ref)

    def logits(k_sl, q_sl):
        return lax.dot_general(k_sl, q_sl, _NT, preferred_element_type=_F32)

    def accumulate(idx, s, v_sl, bias, sel):
        if bias is not None:
            s = s + bias
        if sel is not None:
            s = jnp.where(sel, s, NEG)
        m_prev = m_ref[idx:idx + 1, :]
        m_new = jnp.maximum(m_prev, jnp.max(s, axis=0, keepdims=True))
        p = jnp.exp2(s - m_new)
        alpha = jnp.exp2(m_prev - m_new)
        pv = jnp.dot(v_sl, p.astype(_BF16), preferred_element_type=_F32)
        l_ref[idx:idx + 1, :] = alpha * l_ref[idx:idx + 1, :] + pv[B_V_DIM:B_V_DIM + 1, :]
        acc_ref[idx] = alpha * acc_ref[idx] + pv[:B_V_DIM, :]
        m_ref[idx:idx + 1, :] = m_new

    def process(c, slot, bias_half):
        k0 = pl.multiple_of(c * TK, TK)
        sel = s_ref[pl.ds(k0, TK), :] >= t_row
        rows = None if bias_half is None else slice(bias_half * TK, (bias_half + 1) * TK)
        streams = []
        for h in range(A_HEADS):
            streams.append((h, kabuf, qa_ref, vabuf, h, h, sel))
        for h in range(B_HEADS):
            for half, q_ref in enumerate((qb1_ref, qb2_ref)):
                streams.append((A_HEADS + 2 * h + half, kbbuf, q_ref, vbbuf, h, A_HEADS + h, None))

        def qk(st):
            _, kbuf, q_ref, _, h, _, _ = st
            hs = slice(h * LANES, (h + 1) * LANES)
            return logits(kbuf[slot, :, hs], q_ref[:, hs])

        pending = [qk(st) for st in streams[:QK_LOOKAHEAD]]
        for n, st in enumerate(streams):
            idx, _, _, vbuf, h, bias_head, st_sel = st
            s_cur = pending.pop(0)
            if n + QK_LOOKAHEAD < len(streams):
                pending.append(qk(streams[n + QK_LOOKAHEAD]))
            bias = None if rows is None else bias_ref[bias_head, rows, :]
            accumulate(idx, s_cur, vbuf[slot, h], bias, st_sel)

    def far_body(c, carry):
        slot = lax.rem(c, 2)
        wait(c, slot)
        start(c + 1, 1 - slot)
        process(c, slot, None)
        return carry

    lax.fori_loop(0, jnp.maximum(i - 1, 0), far_body, 0)

    @pl.when(i >= 1)
    def _():
        c = i - 1
        slot = lax.rem(c, 2)
        wait(c, slot)
        start(i, 1 - slot)
        process(c, slot, 0)

    slot = lax.rem(i, 2)
    wait(i, slot)
    process(i, slot, 1)

    for h in range(A_HEADS):
        o_t = acc_ref[h] / l_ref[h:h + 1, :]
        o_ref[:, h * LANES:(h + 1) * LANES] = o_t.T.astype(o_ref.dtype)
    lam = _lambda(lam_ref, lam_init)
    for h in range(B_HEADS):
        i1 = A_HEADS + 2 * h
        d = acc_ref[i1] / l_ref[i1:i1 + 1, :] - lam * (acc_ref[i1 + 1] / l_ref[i1 + 1:i1 + 2, :])
        y = d * lax.rsqrt(jnp.mean(d * d, axis=0, keepdims=True) + SUBLN_EPS) * gsub_ref[...]
        y = y * (1.0 - lam_init)
        off = A_HEADS * A_HEAD_DIM + h * B_V_DIM
        o_ref[:, off:off + B_V_DIM] = y.T.astype(o_ref.dtype)


def _attn_prompt(qa, qi, qb1, qb2, kwt, kwbf, bias, lam_vecs, g_subln, kabf, vat, kbbf, vbt, *, lam_init):
    bsz, t, _ = qa.shape
    n_sel = min(TOPK_MAX, t // 4)
    n_streams = A_HEADS + 2 * B_HEADS

    def qspec(width):
        return pl.BlockSpec((None, TQ, width), lambda b, i: (b, i, 0))

    hbm = pl.BlockSpec(memory_space=pl.ANY)
    return pl.pallas_call(
        functools.partial(_attn_prompt_kernel, n_sel=n_sel, lam_init=lam_init),
        grid=(bsz, t // TQ),
        in_specs=[qspec(512), qspec(1024), qspec(512), qspec(512),
                  pl.BlockSpec((None, LANES, TQ), lambda b, i: (b, 0, i)),
                  pl.BlockSpec((None, t, LANES), lambda b, i: (b, 0, 0)),
                  _resident(bias.shape, lambda b, i: (0, 0, 0)),
                  pl.BlockSpec(lam_vecs.shape, lambda b, i: (0, 0)),
                  pl.BlockSpec((B_V_DIM, 1), lambda b, i: (0, 0)),
                  hbm, hbm, hbm, hbm],
        out_specs=pl.BlockSpec((None, TQ, D_MIX), lambda b, i: (b, i, 0)),
        out_shape=jax.ShapeDtypeStruct((bsz, t, D_MIX), _BF16),
        scratch_shapes=[pltpu.VMEM((t, TQ), jnp.int32),
                        pltpu.VMEM((2, TK, 512), _BF16), pltpu.VMEM((2, A_HEADS, B_V_DIM + ONES_ROWS, TK), _BF16),
                        pltpu.VMEM((2, TK, 512), _BF16), pltpu.VMEM((2, B_HEADS, B_V_DIM + ONES_ROWS, TK), _BF16),
                        pltpu.SemaphoreType.DMA((4, 2)),
                        pltpu.VMEM((n_streams, B_V_DIM, TQ), _F32),
                        pltpu.VMEM((16, TQ), _F32), pltpu.VMEM((16, TQ), _F32)],
        compiler_params=_cparams(("arbitrary", "arbitrary")),
        name="attn_prompt",
    )(qa, qi, qb1, qb2, kwt, kwbf, bias, lam_vecs, g_subln.reshape(B_V_DIM, 1),
      kabf, vat.reshape(bsz, A_HEADS, A_HEAD_DIM, t), kbbf, vbt.reshape(bsz, B_HEADS, B_V_DIM, t))


def _attn_sample_kernel(qa_ref, qi_ref, qb1_ref, qb2_ref, kwn_ref, kan_ref, van_ref, kbn_ref, vbn_ref,
                        bias_p_ref, bias_n_ref, lam_ref, gsub_ref,
                        cidx_ref, cka_ref, cva_ref, ckb_ref, cvb_ref, o_ref,
                        sp_ref, sn_ref, t_ref, acc_ref, m_ref, l_ref,
                        *, n_sel, lam_init, n_new, nk):
    k = pl.program_id(1)
    tk = cka_ref.shape[0] // A_HEADS

    def idx_scores(kidx_bf):
        acc = None
        for h in range(IDX_HEADS):
            q_h = qi_ref[:, h * LANES:h * LANES + IDX_DIM]
            d = lax.dot_general(q_h, kidx_bf, _NT, preferred_element_type=_F32)
            w_h = kwn_ref[0:n_new, IDX_DIM + h:IDX_DIM + h + 1] * IDX_W_SCALE
            term = jnp.maximum(d, 0.0) * w_h
            acc = term if acc is None else acc + term
        return acc

    @pl.when(k == 0)
    def _():
        for j in range(nk):
            kidx = cidx_ref[j * tk:(j + 1) * tk, :].astype(_BF16)
            sp_ref[j] = _order_key(idx_scores(kidx))
        s_new = idx_scores(kwn_ref[:, 0:IDX_DIM].astype(_BF16))
        col = lax.broadcasted_iota(jnp.int32, s_new.shape, 1)
        sn_ref[...] = _order_key(jnp.where(col < n_new, s_new, -jnp.inf))

        def count_ge(cand):
            tot = jnp.sum(jnp.where(sn_ref[...] >= cand, 1, 0), axis=1, keepdims=True)
            for j in range(nk):
                tot = tot + jnp.sum(jnp.where(sp_ref[j] >= cand, 1, 0), axis=1, keepdims=True)
            return tot

        t_ref[...] = _select_threshold(count_ge, (n_new, 1), n_sel, nk * tk + NEW_PAD)
        m_ref[...] = jnp.full(m_ref.shape, NEG, _F32)
        l_ref[...] = jnp.zeros_like(l_ref)
        acc_ref[...] = jnp.zeros_like(acc_ref)

    def stream(idx, q_sl, k_sl, v_sl, bias, sel):
        s = lax.dot_general(q_sl, k_sl, _NT, preferred_element_type=_F32)
        if bias is not None:
            s = s + bias
        if sel is not None:
            s = jnp.where(sel, s, NEG)
        m_prev = m_ref[idx]
        m_new = jnp.maximum(m_prev, jnp.max(s, axis=1, keepdims=True))
        p = jnp.exp2(s - m_new)
        alpha = jnp.exp2(m_prev - m_new)
        l_ref[idx] = alpha * l_ref[idx] + jnp.sum(p, axis=1, keepdims=True)
        acc_ref[idx] = alpha * acc_ref[idx] + jnp.dot(p.astype(_BF16), v_sl, preferred_element_type=_F32)
        m_ref[idx] = m_new

    def process(head_rows, keys, bias_ref):
        sel = keys >= t_ref[...]
        for h in range(A_HEADS):
            hs = slice(h * LANES, (h + 1) * LANES)
            bias = None if bias_ref is None else bias_ref[h]
            stream(h, qa_ref[:, hs], head_rows(0, h), head_rows(1, h), bias, sel)
        for h in range(B_HEADS):
            hs = slice(h * LANES, (h + 1) * LANES)
            bias = None if bias_ref is None else bias_ref[A_HEADS + h]
            kb, vb = head_rows(2, h), head_rows(3, h)
            for half, q_ref in enumerate((qb1_ref, qb2_ref)):
                stream(A_HEADS + 2 * h + half, q_ref[:, hs], kb, vb, bias, None)

    def cached_rows(which, h):
        ref = (cka_ref, cva_ref, ckb_ref, cvb_ref)[which]
        return ref[pl.ds(h, tk, stride=A_HEADS), :].astype(_BF16)

    def new_rows(which, h):
        ref = (kan_ref, van_ref, kbn_ref, vbn_ref)[which]
        return ref[:, h * LANES:(h + 1) * LANES]

    @pl.when(k < nk - 1)
    def _():
        process(cached_rows, sp_ref[k], None)

    @pl.when(k == nk - 1)
    def _():
        process(cached_rows, sp_ref[k], bias_p_ref)
        process(new_rows, sn_ref[...], bias_n_ref)
        for h in range(A_HEADS):
            o_ref[:, h * LANES:(h + 1) * LANES] = (acc_ref[h] / l_ref[h]).astype(o_ref.dtype)
        lam = _lambda(lam_ref, lam_init)
        for h in range(B_HEADS):
            i1 = A_HEADS + 2 * h
            d = acc_ref[i1] / l_ref[i1] - lam * (acc_ref[i1 + 1] / l_ref[i1 + 1])
            y = d * lax.rsqrt(jnp.mean(d * d, axis=1, keepdims=True) + SUBLN_EPS) * gsub_ref[...]
            y = y * (1.0 - lam_init)
            off = A_HEADS * A_HEAD_DIM + h * B_V_DIM
            o_ref[:, off:off + B_V_DIM] = y.astype(o_ref.dtype)


def _attn_sample(qa, qi, qb1, qb2, kw_new, ka_new, va_new, kb_new, vb_new, bias_p, bias_n, lam_vecs, g_subln,
                 c_idx, c_ka, c_va, c_kb, c_vb, *, lam_init):
    bsz, n_new, _ = qa.shape
    past = c_idx.shape[1]
    tk = min(SAMPLE_TK, past)
    nk = past // tk
    n_sel = min(TOPK_MAX, (past + n_new) // 4)
    n_streams = A_HEADS + 2 * B_HEADS

    def per_b(rows, width):
        return pl.BlockSpec((None, rows, width), lambda b, k: (b, 0, 0))

    def cache():
        return pl.BlockSpec((None, tk * A_HEADS, LANES), lambda b, k: (b, k, 0))

    def const(shape):
        return pl.BlockSpec(shape, lambda b, k: (0,) * len(shape))

    return pl.pallas_call(
        functools.partial(_attn_sample_kernel, n_sel=n_sel, lam_init=lam_init, n_new=n_new, nk=nk),
        grid=(bsz, nk),
        in_specs=[per_b(n_new, 512), per_b(n_new, 1024), per_b(n_new, 512), per_b(n_new, 512),
                  per_b(NEW_PAD, LANES), per_b(NEW_PAD, 512), per_b(NEW_PAD, 512), per_b(NEW_PAD, 512),
                  per_b(NEW_PAD, 512),
                  const(bias_p.shape), const(bias_n.shape), const(lam_vecs.shape), const((1, B_V_DIM)),
                  per_b(past, IDX_DIM), cache(), cache(), cache(), cache()],
        out_specs=per_b(n_new, D_MIX),
        out_shape=jax.ShapeDtypeStruct((bsz, n_new, D_MIX), _BF16),
        scratch_shapes=[pltpu.VMEM((nk, n_new, tk), jnp.int32), pltpu.VMEM((n_new, NEW_PAD), jnp.int32),
                        pltpu.VMEM((n_new, 1), jnp.int32),
                        pltpu.VMEM((n_streams, n_new, B_V_DIM), _F32),
                        pltpu.VMEM((n_streams, n_new, 1), _F32), pltpu.VMEM((n_streams, n_new, 1), _F32)],
        compiler_params=_cparams(("parallel", "arbitrary")),
        name="attn_sample",
    )(qa, qi, qb1, qb2, kw_new, ka_new, va_new, kb_new, vb_new, bias_p, bias_n, lam_vecs,
      g_subln.reshape(1, B_V_DIM), c_idx, c_ka, c_va, c_kb, c_vb)


def _outproj_kernel(o_ref, x_ref, gt_ref, gpost_ref, w_ref, y_ref):
    mix = jnp.dot(o_ref[...], w_ref[...], preferred_element_type=_F32)
    y_ref[...] = x_ref[...] + gt_ref[...] * _rms(mix, gpost_ref[...], RMS_EPS)


def _outproj(o, x, gate, g_post, w_out_bf, *, per_token):
    bsz, t, d = x.shape
    tm = min(TOKEN_TILE, t)
    tok = pl.BlockSpec((None, tm, d), lambda b, i: (b, i, 0))
    return pl.pallas_call(
        _outproj_kernel,
        grid=(bsz, t // tm),
        in_specs=[pl.BlockSpec((None, tm, D_MIX), lambda b, i: (b, i, 0)), tok, _mod_specs(per_token, tm, d),
                  pl.BlockSpec((1, d), lambda b, i: (0, 0)),
                  _resident(w_out_bf.shape, lambda b, i: (0, 0))],
        out_specs=tok,
        out_shape=jax.ShapeDtypeStruct(x.shape, _F32),
        compiler_params=_cparams(("parallel", "parallel")),
        name="outproj",
    )(o, x, gate, g_post.reshape(1, d), w_out_bf)


def _pack_w_in(w_in):
    d = w_in.shape[0]
    parts, off = [], 0
    for n in IN_SIZES:
        parts.append(w_in[:, off:off + n])
        off += n
    q_a, k_a, v_a, q_idx, k_idx, w_idx, q_b, k_b, v_b = parts
    zi = jnp.zeros((d, IDX_HEADS, LANES - IDX_DIM), w_in.dtype)
    qi = jnp.concatenate([q_idx.reshape(d, IDX_HEADS, IDX_DIM), zi], axis=-1).reshape(d, IDX_HEADS * LANES)
    qb = q_b.reshape(d, B_HEADS, 2, B_QK_DIM)
    zb = jnp.zeros((d, B_HEADS, B_QK_DIM), w_in.dtype)
    qb1 = jnp.concatenate([qb[:, :, 0], zb], axis=-1).reshape(d, B_HEADS * LANES)
    qb2 = jnp.concatenate([zb, qb[:, :, 1]], axis=-1).reshape(d, B_HEADS * LANES)
    kw = jnp.concatenate([k_idx, w_idx, jnp.zeros((d, LANES - IDX_DIM - IDX_HEADS), w_in.dtype)], axis=-1)
    packed = jnp.concatenate([q_a, qi, qb1, qb2, k_a, v_a, kw, k_b, v_b], axis=-1)
    assert packed.shape[1] == N_PACKED
    return packed.astype(_BF16)


def _pack_ffn(w_gate, w_up, w_down):
    d, f = w_gate.shape
    n = f // FF_CHUNK
    wg3 = w_gate.reshape(d, n, FF_CHUNK).transpose(1, 0, 2).astype(_BF16)
    wu3 = w_up.reshape(d, n, FF_CHUNK).transpose(1, 0, 2).astype(_BF16)
    wd3 = w_down.reshape(n, FF_CHUNK, d).astype(_BF16)
    return wg3, wu3, wd3


def _mods(mod_rows, per_token_len):
    rows = mod_rows.shape[0]
    m = mod_rows.reshape(rows, N_SUB, 3, 1, D_MODEL)
    out = []
    for s in range(N_SUB):
        trip = []
        for kind in range(3):
            v = m[:, s, kind]
            if per_token_len:
                v = jnp.broadcast_to(v, (rows, per_token_len, D_MODEL)).reshape(1, rows * per_token_len, D_MODEL)
            trip.append(v)
        out.append(tuple(trip))
    return out


def _split_heads(x, b, t):
    return x.reshape(1, b, t, -1, LANES)


def kernel(x_prompt, x_sample, cache_a_k, cache_a_v, cache_a_kidx, cache_b_k, cache_b_v, c_prompt, c_sample, rel_bias, w_ada, b_ada, g_pre, g_post, ffn1_w_gate, ffn1_w_up, ffn1_w_down, ffn2_w_gate, ffn2_w_up, ffn2_w_down, w_in, w_out, lambda_q1, lambda_k1, lambda_q2, lambda_k2, g_subln):
    depth = w_in.shape[0]
    assert depth == 1, "stacked caches are returned per layer; only one layer is staged"
    layer = 0
    bp, tp, d = x_prompt.shape
    bs, ts, _ = x_sample.shape
    past = cache_a_k.shape[2]
    assert d == D_MODEL and tp % TQ == 0 and ts <= 16 and past % min(SAMPLE_TK, past) == 0
    lam_init = 0.8 - 0.6 * math.exp(-0.3 * layer)

    n_c = bp + bs
    rows = -(-n_c // 8) * 8
    c_all = jnp.concatenate([c_prompt, c_sample, jnp.zeros((rows - n_c, d), _F32)], axis=0)
    mod = _ada(c_all, w_ada[layer], b_ada[layer])
    mods_p = _mods(mod[:bp], 0)
    mods_s = _mods(mod[bp:bp + bs], ts)

    ffn1 = _pack_ffn(ffn1_w_gate[layer], ffn1_w_up[layer], ffn1_w_down[layer])
    ffn2 = _pack_ffn(ffn2_w_gate[layer], ffn2_w_up[layer], ffn2_w_down[layer])
    w_packed = _pack_w_in(w_in[layer])
    w_out_bf = w_out[layer].astype(_BF16)
    gp, gq = g_pre[layer], g_post[layer]
    lam_vecs = jnp.stack([lambda_q1[layer], lambda_k1[layer], lambda_q2[layer], lambda_k2[layer]], axis=0)
    rel_bias_t = rel_bias.T
    gsub = g_subln[layer]

    (sh, sc, gt) = mods_p[0]
    x1 = _ffn(x_prompt, sh, sc, gt, gp[0], gq[0], *ffn1, per_token=False, res_w=FFN_RES)
    (sh, sc, gt) = mods_p[1]
    (qa, qi, qb1, qb2, ka, va, kw, kb, vb, kabf, vat, kwbf, kwt, kbbf, vbt) = _inproj(
        x1, sh, sc, gp[1], w_packed, per_token=False, transposed=True)
    bias_p = _bias_tiles(rel_bias_t, 2 * TK, TQ, q_on_cols=True, q0=TK, k0=0, n_keys=2 * TK)
    o = _attn_prompt(qa, qi, qb1, qb2, kwt, kwbf, bias_p, lam_vecs, gsub, kabf, vat, kbbf, vbt, lam_init=lam_init)
    x2 = _outproj(o, x1, gt, gq[1], w_out_bf, per_token=False)
    (sh, sc, gt) = mods_p[2]
    y_prompt = _ffn(x2, sh, sc, gt, gp[2], gq[2], *ffn2, per_token=False, res_w=FFN_RES)
    rows_p = (_split_heads(ka, bp, tp), _split_heads(va, bp, tp), kw[..., :IDX_DIM].reshape(1, bp, tp, IDX_DIM),
              _split_heads(kb, bp, tp), _split_heads(vb, bp, tp))

    xs = x_sample.reshape(1, bs * ts, d)
    (sh, sc, gt) = mods_s[0]
    xs1 = _ffn(xs, sh, sc, gt, gp[0], gq[0], *ffn1, per_token=True, res_w=FFN_RES)
    (sh, sc, gt) = mods_s[1]
    (sqa, sqi, sqb1, sqb2, ska, sva, skw, skb, svb, skabf, svabf, skbbf, svbbf) = _inproj(
        xs1, sh, sc, gp[1], w_packed, per_token=True, transposed=False)

    def per_seq(a):
        return a.reshape(bs, ts, a.shape[-1])

    def pad_new(a):
        return jnp.pad(per_seq(a), ((0, 0), (0, NEW_PAD - ts), (0, 0)))

    tk_s = min(SAMPLE_TK, past)
    n_keys = past + ts
    bias_sp = _bias_tiles(rel_bias_t, ts, tk_s, q_on_cols=False, q0=past, k0=past - tk_s, n_keys=n_keys)
    bias_sn = _bias_tiles(rel_bias_t, ts, NEW_PAD, q_on_cols=False, q0=past, k0=past, n_keys=n_keys)
    o_s = _attn_sample(per_seq(sqa), per_seq(sqi), per_seq(sqb1), per_seq(sqb2),
                       pad_new(skw), pad_new(skabf), pad_new(svabf), pad_new(skbbf), pad_new(svbbf),
                       bias_sp, bias_sn, lam_vecs, gsub,
                       cache_a_kidx[layer], cache_a_k[layer].reshape(bs, past * A_HEADS, LANES),
                       cache_a_v[layer].reshape(bs, past * A_HEADS, LANES),
                       cache_b_k[layer].reshape(bs, past * B_HEADS, LANES),
                       cache_b_v[layer].reshape(bs, past * B_HEADS, LANES), lam_init=lam_init)
    xs2 = _outproj(o_s.reshape(1, bs * ts, D_MIX), xs1, gt, gq[1], w_out_bf, per_token=True)
    (sh, sc, gt) = mods_s[2]
    y_sample = _ffn(xs2, sh, sc, gt, gp[2], gq[2], *ffn2, per_token=True, res_w=FFN_RES).reshape(bs, ts, d)
    rows_s = (_split_heads(ska, bs, ts), _split_heads(sva, bs, ts), skw[..., :IDX_DIM].reshape(1, bs, ts, IDX_DIM),
              _split_heads(skb, bs, ts), _split_heads(svb, bs, ts))

    return (y_prompt, y_sample) + rows_p + rows_s
```

```python
import functools
import math

import jax
import jax.numpy as jnp
from jax import lax
from jax.experimental import pallas as pl
from jax.experimental.pallas import tpu as pltpu

D_MODEL = 1024
CHUNK = 64
A_HEADS = 4
A_HEAD_DIM = 128
IDX_HEADS = 8
IDX_DIM = 64
TOPK_MAX = 256
B_HEADS = 4
B_QK_DIM = 64
B_V_DIM = 128
D_MIX = A_HEADS * A_HEAD_DIM + B_HEADS * B_V_DIM
D_FF = 2816
FFN_RES = 0.5
N_BUCKETS = 32
MAX_DISTANCE = 128
N_SUB = 3
RMS_EPS = 1e-6
SUBLN_EPS = 1e-5
IN_SIZES = (512, 512, 512, 512, 64, 8, 512, 512, 512)

LANES = 128
VMEM_LIMIT_BYTES = 56 * 1024 * 1024

FF_CHUNK = 256
TOKEN_TILE = 512
TQ = 256
TK = 256
SAMPLE_TK = 1024
NEW_PAD = 128
QK_LOOKAHEAD = 12
COUNT_ROWS = 32

NEG = -1e30
LOG2E = math.log2(math.e)
A_SCALE = A_HEAD_DIM ** -0.5 * LOG2E
B_SCALE = B_QK_DIM ** -0.5 * LOG2E
ONES_ROWS = 16
IDX_SCALE = IDX_DIM ** -0.5
IDX_W_SCALE = IDX_HEADS ** -0.5
INT_MIN = -(2 ** 31)

SEG_QA, SEG_QI, SEG_QB1, SEG_QB2, SEG_KA, SEG_VA, SEG_KW, SEG_KB, SEG_VB = (
    (0, 512), (512, 1024), (1536, 512), (2048, 512), (2560, 512), (3072, 512), (3584, 128), (3712, 512), (4224, 512))
N_PACKED = 4736

_NT = (((1,), (1,)), ((), ()))
_F32 = jnp.float32
_BF16 = jnp.bfloat16


def _cparams(sem):
    return pltpu.CompilerParams(dimension_semantics=sem, vmem_limit_bytes=VMEM_LIMIT_BYTES)


def _resident(shape, index_map):
    return pl.BlockSpec(shape, index_map, pipeline_mode=pl.Buffered(1))


def _ada_kernel(c_ref, w_ref, b_ref, o_ref):
    c = c_ref[...]
    s = c * jax.nn.sigmoid(c)
    o_ref[...] = jnp.dot(s, w_ref[...], precision=lax.Precision.HIGHEST,
                         preferred_element_type=_F32) + b_ref[...]


def _ada(c, w_ada, b_ada):
    rows, d = c.shape
    n = w_ada.shape[1]
    tn = 1024
    return pl.pallas_call(
        _ada_kernel,
        grid=(n // tn,),
        in_specs=[pl.BlockSpec((rows, d), lambda j: (0, 0)),
                  pl.BlockSpec((d, tn), lambda j: (0, j)),
                  pl.BlockSpec((1, tn), lambda j: (0, j))],
        out_specs=pl.BlockSpec((rows, tn), lambda j: (0, j)),
        out_shape=jax.ShapeDtypeStruct((rows, n), _F32),
        compiler_params=_cparams(("parallel",)),
        name="ada_mod",
    )(c, w_ada, b_ada.reshape(1, n))


def _rms(x, g, eps):
    return x * lax.rsqrt(jnp.mean(x * x, axis=-1, keepdims=True) + eps) * g


def _mod_specs(per_token, tm, d):
    if per_token:
        return pl.BlockSpec((None, tm, d), lambda b, t: (b, t, 0))
    return pl.BlockSpec((None, 1, d), lambda b, t: (b, 0, 0))


def _ffn_kernel(x_ref, sh_ref, sc_ref, gt_ref, gpre_ref, gpost_ref, wg_ref, wu_ref, wd_ref,
                o_ref, h_ref, acc_ref, *, n_chunks, res_w):
    x = x_ref[...]
    h = _rms(x, gpre_ref[...], RMS_EPS) * (1.0 + sc_ref[...]) + sh_ref[...]
    h_ref[...] = h.astype(_BF16)
    acc_ref[...] = jnp.zeros_like(acc_ref)

    def body(c, carry):
        hb = h_ref[...]
        g = jnp.dot(hb, wg_ref[c], preferred_element_type=_F32)
        u = jnp.dot(hb, wu_ref[c], preferred_element_type=_F32)
        a = (g * jax.nn.sigmoid(g)) * u
        acc_ref[...] += jnp.dot(a.astype(_BF16), wd_ref[c], preferred_element_type=_F32)
        return carry

    lax.fori_loop(0, n_chunks, body, 0)
    y = _rms(acc_ref[...], gpost_ref[...], RMS_EPS)
    o_ref[...] = x + res_w * gt_ref[...] * y


def _ffn(x, shift, scale, gate, g_pre, g_post, wg3, wu3, wd3, *, per_token, res_w):
    bsz, t, d = x.shape
    tm = min(TOKEN_TILE, t)
    n_chunks = wg3.shape[0]
    tok = pl.BlockSpec((None, tm, d), lambda b, i: (b, i, 0))
    vec = pl.BlockSpec((1, d), lambda b, i: (0, 0))
    mod = _mod_specs(per_token, tm, d)
    return pl.pallas_call(
        functools.partial(_ffn_kernel, n_chunks=n_chunks, res_w=res_w),
        grid=(bsz, t // tm),
        in_specs=[tok, mod, mod, mod, vec, vec,
                  _resident(wg3.shape, lambda b, i: (0, 0, 0)),
                  _resident(wu3.shape, lambda b, i: (0, 0, 0)),
                  _resident(wd3.shape, lambda b, i: (0, 0, 0))],
        out_specs=tok,
        out_shape=jax.ShapeDtypeStruct(x.shape, _F32),
        scratch_shapes=[pltpu.VMEM((tm, d), _BF16), pltpu.VMEM((tm, d), _F32)],
        compiler_params=_cparams(("parallel", "parallel")),
        name="ffn",
    )(x, shift, scale, gate, g_pre.reshape(1, d), g_post.reshape(1, d), wg3, wu3, wd3)


def _inproj_kernel(x_ref, sh_ref, sc_ref, gpre_ref, w_ref, *out_refs, transposed):
    if transposed:
        (qa_ref, qi_ref, qb1_ref, qb2_ref, ka_ref, va_ref, kw_ref, kb_ref, vb_ref,
         kabf_ref, vat_ref, kwbf_ref, kwt_ref, kbbf_ref, vbt_ref) = out_refs
    else:
        (qa_ref, qi_ref, qb1_ref, qb2_ref, ka_ref, va_ref, kw_ref, kb_ref, vb_ref,
         kabf_ref, vabf_ref, kbbf_ref, vbbf_ref) = out_refs
    h = _rms(x_ref[...], gpre_ref[...], RMS_EPS) * (1.0 + sc_ref[...]) + sh_ref[...]
    hb = h.astype(_BF16)

    def proj(seg):
        off, width = seg
        return jnp.dot(hb, w_ref[:, off:off + width], preferred_element_type=_F32)

    qa_ref[...] = (proj(SEG_QA) * A_SCALE).astype(_BF16)
    qi_ref[...] = (proj(SEG_QI) * IDX_SCALE).astype(_BF16)
    qb1_ref[...] = (proj(SEG_QB1) * B_SCALE).astype(_BF16)
    qb2_ref[...] = (proj(SEG_QB2) * B_SCALE).astype(_BF16)
    tm = x_ref.shape[0]

    def store_rows(ref, val, heads):
        for hd in range(heads):
            ref[pl.ds(hd, tm, stride=heads), :] = val[:, hd * LANES:(hd + 1) * LANES]

    ka = proj(SEG_KA)
    store_rows(ka_ref, ka, A_HEADS)
    kabf_ref[...] = ka.astype(_BF16)
    kb = proj(SEG_KB)
    store_rows(kb_ref, kb, B_HEADS)
    kbbf_ref[...] = kb.astype(_BF16)
    kw = proj(SEG_KW)
    kw_ref[...] = kw
    va = proj(SEG_VA)
    store_rows(va_ref, va, A_HEADS)
    vb = proj(SEG_VB)
    store_rows(vb_ref, vb, B_HEADS)
    if transposed:
        kwbf_ref[...] = kw.astype(_BF16)
        kwt_ref[...] = kw.T
        vat_ref[...] = va.T.astype(_BF16)
        vbt_ref[...] = vb.T.astype(_BF16)
    else:
        vabf_ref[...] = va.astype(_BF16)
        vbbf_ref[...] = vb.astype(_BF16)


def _inproj(x, shift, scale, g_pre, w_packed, *, per_token, transposed):
    bsz, t, d = x.shape
    tm = min(TOKEN_TILE, t)

    def tok(width, dtype):
        return (pl.BlockSpec((None, tm, width), lambda b, i: (b, i, 0)),
                jax.ShapeDtypeStruct((bsz, t, width), dtype))

    def tr(width, dtype):
        return (pl.BlockSpec((None, width, tm), lambda b, i: (b, 0, i)),
                jax.ShapeDtypeStruct((bsz, width, t), dtype))

    def rows(heads):
        return (pl.BlockSpec((None, tm * heads, LANES), lambda b, i: (b, i, 0)),
                jax.ShapeDtypeStruct((bsz, t * heads, LANES), _F32))

    outs = [tok(512, _BF16), tok(1024, _BF16), tok(512, _BF16), tok(512, _BF16),
            rows(A_HEADS), rows(A_HEADS), tok(LANES, _F32), rows(B_HEADS), rows(B_HEADS)]
    if transposed:
        outs += [tok(512, _BF16), tr(512, _BF16), tok(LANES, _BF16), tr(LANES, _F32),
                 tok(512, _BF16), tr(512, _BF16)]
    else:
        outs += [tok(512, _BF16), tok(512, _BF16), tok(512, _BF16), tok(512, _BF16)]
    mod = _mod_specs(per_token, tm, d)
    return pl.pallas_call(
        functools.partial(_inproj_kernel, transposed=transposed),
        grid=(bsz, t // tm),
        in_specs=[pl.BlockSpec((None, tm, d), lambda b, i: (b, i, 0)), mod, mod,
                  pl.BlockSpec((1, d), lambda b, i: (0, 0)),
                  _resident(w_packed.shape, lambda b, i: (0, 0))],
        out_specs=[o[0] for o in outs],
        out_shape=[o[1] for o in outs],
        compiler_params=_cparams(("parallel", "parallel")),
        name="inproj",
    )(x, shift, scale, g_pre.reshape(1, d), w_packed)


def _bucket_thresholds():
    nb = N_BUCKETS // 2
    max_exact = nb // 2
    ratio = MAX_DISTANCE / max_exact
    out = []
    for j in range(1, nb - max_exact):
        out.append(int(math.ceil(max_exact * ratio ** (j / (nb - max_exact)) - 1e-9)))
    return nb, max_exact, tuple(out)


def _bias_tile_kernel(tab_ref, o_ref, *, q_on_cols, q0, k0, n_keys):
    hh = pl.program_id(0)
    shape = o_ref.shape
    r = lax.broadcasted_iota(jnp.int32, shape, 0)
    c = lax.broadcasted_iota(jnp.int32, shape, 1)
    qpos = q0 + (c if q_on_cols else r)
    kpos = k0 + (r if q_on_cols else c)
    rel = kpos - qpos
    n = jnp.abs(rel)
    nb, max_exact, thresholds = _bucket_thresholds()
    large = jnp.full(shape, max_exact, jnp.int32)
    for th in thresholds:
        large = large + jnp.where(n >= th, 1, 0)
    bucket = jnp.where(rel > 0, nb, 0) + jnp.where(n < max_exact, n, large)
    val = jnp.zeros(shape, _F32)
    for bkt in range(N_BUCKETS):
        val = jnp.where(bucket == bkt, tab_ref[hh, bkt], val)
    val = (val - tab_ref[hh, nb - 1]) * LOG2E
    chunk_end = (lax.shift_right_arithmetic(qpos, int(math.log2(CHUNK))) + 1) * CHUNK
    admit = (kpos < chunk_end) & (kpos >= 0) & (kpos < n_keys)
    o_ref[...] = jnp.where(admit, val, NEG)


def _bias_tiles(rel_bias_t, rows, cols, *, q_on_cols, q0, k0, n_keys):
    n_heads = rel_bias_t.shape[0]
    return pl.pallas_call(
        functools.partial(_bias_tile_kernel, q_on_cols=q_on_cols, q0=q0, k0=k0, n_keys=n_keys),
        grid=(n_heads,),
        in_specs=[pl.BlockSpec(memory_space=pltpu.SMEM)],
        out_specs=pl.BlockSpec((None, rows, cols), lambda h: (h, 0, 0)),
        out_shape=jax.ShapeDtypeStruct((n_heads, rows, cols), _F32),
        compiler_params=_cparams(("parallel",)),
        name="bias_tiles",
    )(rel_bias_t)


def _order_key(score):
    bits = lax.bitcast_convert_type(score, jnp.int32)
    return bits ^ (lax.shift_right_arithmetic(bits, 31) & 0x7FFFFFFF)


def _select_threshold(count_ge, shape, n_sel, n_total):
    def cond(carry):
        it, _, _, worst = carry
        return (it < 32) & (worst > n_sel)

    def body(carry):
        it, t_u, c_t, _ = carry
        worst = jnp.max(c_t)
        bit = lax.shift_left(jnp.int32(1), 31 - it)
        cand_u = t_u | bit
        cnt = count_ge(cand_u ^ INT_MIN)
        take = cnt >= n_sel
        t_u = jnp.where(take, cand_u, t_u)
        c_t = jnp.where(take, cnt, c_t)
        return it + 1, t_u, c_t, worst

    n_total = jnp.asarray(n_total, jnp.int32)
    init = (jnp.int32(0), jnp.zeros(shape, jnp.int32), jnp.full(shape, n_total, jnp.int32), n_total)
    _, t_u, _, _ = lax.while_loop(cond, body, init)
    return t_u ^ INT_MIN


def _lambda(lam_ref, lam_init):
    l1 = jnp.sum(lam_ref[0:1, :] * lam_ref[1:2, :], axis=-1, keepdims=True)
    l2 = jnp.sum(lam_ref[2:3, :] * lam_ref[3:4, :], axis=-1, keepdims=True)
    return jnp.exp(l1) - jnp.exp(l2) + lam_init


def _attn_prompt_kernel(qa_ref, qi_ref, qb1_ref, qb2_ref, kwt_ref, kw_ref, bias_ref, lam_ref, gsub_ref,
                        ka_hbm, vat_hbm, kb_hbm, vbt_hbm, o_ref,
                        s_ref, kabuf, vabuf, kbbuf, vbbuf, sem, acc_ref, m_ref, l_ref,
                        *, n_sel, lam_init):
    b = pl.program_id(0)
    i = pl.program_id(1)

    def copies(c, slot):
        k0 = pl.multiple_of(c * TK, TK)
        vrows = pl.ds(0, B_V_DIM)
        return (pltpu.make_async_copy(ka_hbm.at[b, pl.ds(k0, TK), :], kabuf.at[slot], sem.at[0, slot]),
                pltpu.make_async_copy(vat_hbm.at[b, :, :, pl.ds(k0, TK)], vabuf.at[slot, :, vrows, :], sem.at[1, slot]),
                pltpu.make_async_copy(kb_hbm.at[b, pl.ds(k0, TK), :], kbbuf.at[slot], sem.at[2, slot]),
                pltpu.make_async_copy(vbt_hbm.at[b, :, :, pl.ds(k0, TK)], vbbuf.at[slot, :, vrows, :], sem.at[3, slot]))

    def start(c, slot):
        for cp in copies(c, slot):
            cp.start()

    def wait(c, slot):
        for cp in copies(c, slot):
            cp.wait()

    start(0, 0)
    ones = jnp.ones((2, A_HEADS, ONES_ROWS, TK), _BF16)
    vabuf[:, :, B_V_DIM:, :] = ones
    vbbuf[:, :, B_V_DIM:, :] = ones

    w_rows = kwt_ref[IDX_DIM:IDX_DIM + IDX_HEADS, :] * IDX_W_SCALE
    rr = lax.broadcasted_iota(jnp.int32, (TK, TQ), 0)
    cc = lax.broadcasted_iota(jnp.int32, (TK, TQ), 1)
    admit_diag = rr < (lax.shift_right_arithmetic(cc, int(math.log2(CHUNK))) + 1) * CHUNK

    def score_chunk(c, diagonal):
        k0 = pl.multiple_of(c * TK, TK)
        kblk = kw_ref[pl.ds(k0, TK), :]
        acc = None
        for h in range(IDX_HEADS):
            d = lax.dot_general(kblk, qi_ref[:, h * LANES:(h + 1) * LANES], _NT, preferred_element_type=_F32)
            term = jnp.maximum(d, 0.0) * w_rows[h:h + 1, :]
            acc = term if acc is None else acc + term
        if diagonal:
            acc = jnp.where(admit_diag, acc, -jnp.inf)
        s_ref[pl.ds(k0, TK), :] = _order_key(acc)

    def far_scores(c, carry):
        score_chunk(c, False)
        return carry

    lax.fori_loop(0, i, far_scores, 0)
    score_chunk(i, True)

    def count_ge(cand):
        def hits(k0, rows, one):
            hit = jnp.where(s_ref[pl.ds(k0, rows), :] >= cand, one, 0)
            return jnp.sum(hit.reshape(rows // COUNT_ROWS, COUNT_ROWS, TQ), axis=0)

        def body(c, acc):
            return acc + hits(pl.multiple_of(c * (2 * TK), 2 * TK), 2 * TK, 1)

        n_chunks = i + 1
        part = lax.fori_loop(0, lax.shift_right_logical(n_chunks, 1), body,
                             jnp.zeros((COUNT_ROWS, TQ), jnp.int32))
        part = part + hits(pl.multiple_of(i * TK, TK), TK, n_chunks & 1)
        return jnp.sum(part, axis=0, keepdims=True)

    t_row = _select_threshold(count_ge, (1, TQ), n_sel, (i + 1) * TK)

    m_ref[...] = jnp.full(m_ref.shape, NEG, _F32)
    l_ref[...] = jnp.zeros_like(l_ref)
    acc_ref[...] = jnp.zeros_like(acc_ref)

    def logits(k_sl, q_sl):
        return lax.dot_general(k_sl, q_sl, _NT, preferred_element_type=_F32)

    def accumulate(idx, s, v_sl, bias, sel):
        if bias is not None:
            s = s + bias
        if sel is not None:
            s = jnp.where(sel, s, NEG)
        m_prev = m_ref[idx:idx + 1, :]
        m_new = jnp.maximum(m_prev, jnp.max(s, axis=0, keepdims=True))
        p = jnp.exp2(s - m_new)
        alpha = jnp.exp2(m_prev - m_new)
        pv = jnp.dot(v_sl, p.astype(_BF16), preferred_element_type=_F32)
        l_ref[idx:idx + 1, :] = alpha * l_ref[idx:idx + 1, :] + pv[B_V_DIM:B_V_DIM + 1, :]
        acc_ref[idx] = alpha * acc_ref[idx] + pv[:B_V_DIM, :]
        m_ref[idx:idx + 1, :] = m_new

    def process(c, slot, bias_half):
        k0 = pl.multiple_of(c * TK, TK)
        sel = s_ref[pl.ds(k0, TK), :] >= t_row
        rows = None if bias_half is None else slice(bias_half * TK, (bias_half + 1) * TK)
        streams = []
        for h in range(A_HEADS):
            streams.append((h, kabuf, qa_ref, vabuf, h, h, sel))
        for h in range(B_HEADS):
            for half, q_ref in enumerate((qb1_ref, qb2_ref)):
                streams.append((A_HEADS + 2 * h + half, kbbuf, q_ref, vbbuf, h, A_HEADS + h, None))

        def qk(st):
            _, kbuf, q_ref, _, h, _, _ = st
            hs = slice(h * LANES, (h + 1) * LANES)
            return logits(kbuf[slot, :, hs], q_ref[:, hs])

        pending = [qk(st) for st in streams[:QK_LOOKAHEAD]]
        for n, st in enumerate(streams):
            idx, _, _, vbuf, h, bias_head, st_sel = st
            s_cur = pending.pop(0)
            if n + QK_LOOKAHEAD < len(streams):
                pending.append(qk(streams[n + QK_LOOKAHEAD]))
            bias = None if rows is None else bias_ref[bias_head, rows, :]
            accumulate(idx, s_cur, vbuf[slot, h], bias, st_sel)

    def far_body(c, carry):
        slot = lax.rem(c, 2)
        wait(c, slot)
        start(c + 1, 1 - slot)
        process(c, slot, None)
        return carry

    lax.fori_loop(0, jnp.maximum(i - 1, 0), far_body, 0)

    @pl.when(i >= 1)
    def _():
        c = i - 1
        slot = lax.rem(c, 2)
        wait(c, slot)
        start(i, 1 - slot)
        process(c, slot, 0)

    slot = lax.rem(i, 2)
    wait(i, slot)
    process(i, slot, 1)

    for h in range(A_HEADS):
        o_t = acc_ref[h] / l_ref[h:h + 1, :]
        o_ref[:, h * LANES:(h + 1) * LANES] = o_t.T.astype(o_ref.dtype)
    lam = _lambda(lam_ref, lam_init)
    for h in range(B_HEADS):
        i1 = A_HEADS + 2 * h
        d = acc_ref[i1] / l_ref[i1:i1 + 1, :] - lam * (acc_ref[i1 + 1] / l_ref[i1 + 1:i1 + 2, :])
        y = d * lax.rsqrt(jnp.mean(d * d, axis=0, keepdims=True) + SUBLN_EPS) * gsub_ref[...]
        y = y * (1.0 - lam_init)
        off = A_HEADS * A_HEAD_DIM + h * B_V_DIM
        o_ref[:, off:off + B_V_DIM] = y.T.astype(o_ref.dtype)


def _attn_prompt(qa, qi, qb1, qb2, kwt, kwbf, bias, lam_vecs, g_subln, kabf, vat, kbbf, vbt, *, lam_init):
    bsz, t, _ = qa.shape
    n_sel = min(TOPK_MAX, t // 4)
    n_streams = A_HEADS + 2 * B_HEADS

    def qspec(width):
        return pl.BlockSpec((None, TQ, width), lambda b, i: (b, i, 0))

    hbm = pl.BlockSpec(memory_space=pl.ANY)
    return pl.pallas_call(
        functools.partial(_attn_prompt_kernel, n_sel=n_sel, lam_init=lam_init),
        grid=(bsz, t // TQ),
        in_specs=[qspec(512), qspec(1024), qspec(512), qspec(512),
                  pl.BlockSpec((None, LANES, TQ), lambda b, i: (b, 0, i)),
                  pl.BlockSpec((None, t, LANES), lambda b, i: (b, 0, 0)),
                  _resident(bias.shape, lambda b, i: (0, 0, 0)),
                  pl.BlockSpec(lam_vecs.shape, lambda b, i: (0, 0)),
                  pl.BlockSpec((B_V_DIM, 1), lambda b, i: (0, 0)),
                  hbm, hbm, hbm, hbm],
        out_specs=pl.BlockSpec((None, TQ, D_MIX), lambda b, i: (b, i, 0)),
        out_shape=jax.ShapeDtypeStruct((bsz, t, D_MIX), _BF16),
        scratch_shapes=[pltpu.VMEM((t, TQ), jnp.int32),
                        pltpu.VMEM((2, TK, 512), _BF16), pltpu.VMEM((2, A_HEADS, B_V_DIM + ONES_ROWS, TK), _BF16),
                        pltpu.VMEM((2, TK, 512), _BF16), pltpu.VMEM((2, B_HEADS, B_V_DIM + ONES_ROWS, TK), _BF16),
                        pltpu.SemaphoreType.DMA((4, 2)),
                        pltpu.VMEM((n_streams, B_V_DIM, TQ), _F32),
                        pltpu.VMEM((16, TQ), _F32), pltpu.VMEM((16, TQ), _F32)],
        compiler_params=_cparams(("arbitrary", "arbitrary")),
        name="attn_prompt",
    )(qa, qi, qb1, qb2, kwt, kwbf, bias, lam_vecs, g_subln.reshape(B_V_DIM, 1),
      kabf, vat.reshape(bsz, A_HEADS, A_HEAD_DIM, t), kbbf, vbt.reshape(bsz, B_HEADS, B_V_DIM, t))


def _attn_sample_kernel(qa_ref, qi_ref, qb1_ref, qb2_ref, kwn_ref, kan_ref, van_ref, kbn_ref, vbn_ref,
                        bias_p_ref, bias_n_ref, lam_ref, gsub_ref,
                        cidx_ref, cka_ref, cva_ref, ckb_ref, cvb_ref, o_ref,
                        sp_ref, sn_ref, t_ref, acc_ref, m_ref, l_ref,
                        *, n_sel, lam_init, n_new, nk):
    k = pl.program_id(1)
    tk = cka_ref.shape[0] // A_HEADS

    def idx_scores(kidx_bf):
        acc = None
        for h in range(IDX_HEADS):
            q_h = qi_ref[:, h * LANES:h * LANES + IDX_DIM]
            d = lax.dot_general(q_h, kidx_bf, _NT, preferred_element_type=_F32)
            w_h = kwn_ref[0:n_new, IDX_DIM + h:IDX_DIM + h + 1] * IDX_W_SCALE
            term = jnp.maximum(d, 0.0) * w_h
            acc = term if acc is None else acc + term
        return acc

    @pl.when(k == 0)
    def _():
        for j in range(nk):
            kidx = cidx_ref[j * tk:(j + 1) * tk, :].astype(_BF16)
            sp_ref[j] = _order_key(idx_scores(kidx))
        s_new = idx_scores(kwn_ref[:, 0:IDX_DIM].astype(_BF16))
        col = lax.broadcasted_iota(jnp.int32, s_new.shape, 1)
        sn_ref[...] = _order_key(jnp.where(col < n_new, s_new, -jnp.inf))

        def count_ge(cand):
            tot = jnp.sum(jnp.where(sn_ref[...] >= cand, 1, 0), axis=1, keepdims=True)
            for j in range(nk):
                tot = tot + jnp.sum(jnp.where(sp_ref[j] >= cand, 1, 0), axis=1, keepdims=True)
            return tot

        t_ref[...] = _select_threshold(count_ge, (n_new, 1), n_sel, nk * tk + NEW_PAD)
        m_ref[...] = jnp.full(m_ref.shape, NEG, _F32)
        l_ref[...] = jnp.zeros_like(l_ref)
        acc_ref[...] = jnp.zeros_like(acc_ref)

    def stream(idx, q_sl, k_sl, v_sl, bias, sel):
        s = lax.dot_general(q_sl, k_sl, _NT, preferred_element_type=_F32)
        if bias is not None:
            s = s + bias
        if sel is not None:
            s = jnp.where(sel, s, NEG)
        m_prev = m_ref[idx]
        m_new = jnp.maximum(m_prev, jnp.max(s, axis=1, keepdims=True))
        p = jnp.exp2(s - m_new)
        alpha = jnp.exp2(m_prev - m_new)
        l_ref[idx] = alpha * l_ref[idx] + jnp.sum(p, axis=1, keepdims=True)
        acc_ref[idx] = alpha * acc_ref[idx] + jnp.dot(p.astype(_BF16), v_sl, preferred_element_type=_F32)
        m_ref[idx] = m_new

    def process(head_rows, keys, bias_ref):
        sel = keys >= t_ref[...]
        for h in range(A_HEADS):
            hs = slice(h * LANES, (h + 1) * LANES)
            bias = None if bias_ref is None else bias_ref[h]
            stream(h, qa_ref[:, hs], head_rows(0, h), head_rows(1, h), bias, sel)
        for h in range(B_HEADS):
            hs = slice(h * LANES, (h + 1) * LANES)
            bias = None if bias_ref is None else bias_ref[A_HEADS + h]
            kb, vb = head_rows(2, h), head_rows(3, h)
            for half, q_ref in enumerate((qb1_ref, qb2_ref)):
                stream(A_HEADS + 2 * h + half, q_ref[:, hs], kb, vb, bias, None)

    def cached_rows(which, h):
        ref = (cka_ref, cva_ref, ckb_ref, cvb_ref)[which]
        return ref[pl.ds(h, tk, stride=A_HEADS), :].astype(_BF16)

    def new_rows(which, h):
        ref = (kan_ref, van_ref, kbn_ref, vbn_ref)[which]
        return ref[:, h * LANES:(h + 1) * LANES]

    @pl.when(k < nk - 1)
    def _():
        process(cached_rows, sp_ref[k], None)

    @pl.when(k == nk - 1)
    def _():
        process(cached_rows, sp_ref[k], bias_p_ref)
        process(new_rows, sn_ref[...], bias_n_ref)
        for h in range(A_HEADS):
            o_ref[:, h * LANES:(h + 1) * LANES] = (acc_ref[h] / l_ref[h]).astype(o_ref.dtype)
        lam = _lambda(lam_ref, lam_init)
        for h in range(B_HEADS):
            i1 = A_HEADS + 2 * h
            d = acc_ref[i1] / l_ref[i1] - lam * (acc_ref[i1 + 1] / l_ref[i1 + 1])
            y = d * lax.rsqrt(jnp.mean(d * d, axis=1, keepdims=True) + SUBLN_EPS) * gsub_ref[...]
            y = y * (1.0 - lam_init)
            off = A_HEADS * A_HEAD_DIM + h * B_V_DIM
            o_ref[:, off:off + B_V_DIM] = y.astype(o_ref.dtype)


def _attn_sample(qa, qi, qb1, qb2, kw_new, ka_new, va_new, kb_new, vb_new, bias_p, bias_n, lam_vecs, g_subln,
                 c_idx, c_ka, c_va, c_kb, c_vb, *, lam_init):
    bsz, n_new, _ = qa.shape
    past = c_idx.shape[1]
    tk = min(SAMPLE_TK, past)
    nk = past // tk
    n_sel = min(TOPK_MAX, (past + n_new) // 4)
    n_streams = A_HEADS + 2 * B_HEADS

    def per_b(rows, width):
        return pl.BlockSpec((None, rows, width), lambda b, k: (b, 0, 0))

    def cache():
        return pl.BlockSpec((None, tk * A_HEADS, LANES), lambda b, k: (b, k, 0))

    def const(shape):
        return pl.BlockSpec(shape, lambda b, k: (0,) * len(shape))

    return pl.pallas_call(
        functools.partial(_attn_sample_kernel, n_sel=n_sel, lam_init=lam_init, n_new=n_new, nk=nk),
        grid=(bsz, nk),
        in_specs=[per_b(n_new, 512), per_b(n_new, 1024), per_b(n_new, 512), per_b(n_new, 512),
                  per_b(NEW_PAD, LANES), per_b(NEW_PAD, 512), per_b(NEW_PAD, 512), per_b(NEW_PAD, 512),
                  per_b(NEW_PAD, 512),
                  const(bias_p.shape), const(bias_n.shape), const(lam_vecs.shape), const((1, B_V_DIM)),
                  per_b(past, IDX_DIM), cache(), cache(), cache(), cache()],
        out_specs=per_b(n_new, D_MIX),
        out_shape=jax.ShapeDtypeStruct((bsz, n_new, D_MIX), _BF16),
        scratch_shapes=[pltpu.VMEM((nk, n_new, tk), jnp.int32), pltpu.VMEM((n_new, NEW_PAD), jnp.int32),
                        pltpu.VMEM((n_new, 1), jnp.int32),
                        pltpu.VMEM((n_streams, n_new, B_V_DIM), _F32),
                        pltpu.VMEM((n_streams, n_new, 1), _F32), pltpu.VMEM((n_streams, n_new, 1), _F32)],
        compiler_params=_cparams(("parallel", "arbitrary")),
        name="attn_sample",
    )(qa, qi, qb1, qb2, kw_new, ka_new, va_new, kb_new, vb_new, bias_p, bias_n, lam_vecs,
      g_subln.reshape(1, B_V_DIM), c_idx, c_ka, c_va, c_kb, c_vb)


def _outproj_kernel(o_ref, x_ref, gt_ref, gpost_ref, w_ref, y_ref):
    mix = jnp.dot(o_ref[...], w_ref[...], preferred_element_type=_F32)
    y_ref[...] = x_ref[...] + gt_ref[...] * _rms(mix, gpost_ref[...], RMS_EPS)


def _outproj(o, x, gate, g_post, w_out_bf, *, per_token):
    bsz, t, d = x.shape
    tm = min(TOKEN_TILE, t)
    tok = pl.BlockSpec((None, tm, d), lambda b, i: (b, i, 0))
    return pl.pallas_call(
        _outproj_kernel,
        grid=(bsz, t // tm),
        in_specs=[pl.BlockSpec((None, tm, D_MIX), lambda b, i: (b, i, 0)), tok, _mod_specs(per_token, tm, d),
                  pl.BlockSpec((1, d), lambda b, i: (0, 0)),
                  _resident(w_out_bf.shape, lambda b, i: (0, 0))],
        out_specs=tok,
        out_shape=jax.ShapeDtypeStruct(x.shape, _F32),
        compiler_params=_cparams(("parallel", "parallel")),
        name="outproj",
    )(o, x, gate, g_post.reshape(1, d), w_out_bf)


def _pack_w_in(w_in):
    d = w_in.shape[0]
    parts, off = [], 0
    for n in IN_SIZES:
        parts.append(w_in[:, off:off + n])
        off += n
    q_a, k_a, v_a, q_idx, k_idx, w_idx, q_b, k_b, v_b = parts
    zi = jnp.zeros((d, IDX_HEADS, LANES - IDX_DIM), w_in.dtype)
    qi = jnp.concatenate([q_idx.reshape(d, IDX_HEADS, IDX_DIM), zi], axis=-1).reshape(d, IDX_HEADS * LANES)
    qb = q_b.reshape(d, B_HEADS, 2, B_QK_DIM)
    zb = jnp.zeros((d, B_HEADS, B_QK_DIM), w_in.dtype)
    qb1 = jnp.concatenate([qb[:, :, 0], zb], axis=-1).reshape(d, B_HEADS * LANES)
    qb2 = jnp.concatenate([zb, qb[:, :, 1]], axis=-1).reshape(d, B_HEADS * LANES)
    kw = jnp.concatenate([k_idx, w_idx, jnp.zeros((d, LANES - IDX_DIM - IDX_HEADS), w_in.dtype)], axis=-1)
    packed = jnp.concatenate([q_a, qi, qb1, qb2, k_a, v_a, kw, k_b, v_b], axis=-1)
    assert packed.shape[1] == N_PACKED
    return packed.astype(_BF16)


def _pack_ffn(w_gate, w_up, w_down):
    d, f = w_gate.shape
    n = f // FF_CHUNK
    wg3 = w_gate.reshape(d, n, FF_CHUNK).transpose(1, 0, 2).astype(_BF16)
    wu3 = w_up.reshape(d, n, FF_CHUNK).transpose(1, 0, 2).astype(_BF16)
    wd3 = w_down.reshape(n, FF_CHUNK, d).astype(_BF16)
    return wg3, wu3, wd3


def _mods(mod_rows, per_token_len):
    rows = mod_rows.shape[0]
    m = mod_rows.reshape(rows, N_SUB, 3, 1, D_MODEL)
    out = []
    for s in range(N_SUB):
        trip = []
        for kind in range(3):
            v = m[:, s, kind]
            if per_token_len:
                v = jnp.broadcast_to(v, (rows, per_token_len, D_MODEL)).reshape(1, rows * per_token_len, D_MODEL)
            trip.append(v)
        out.append(tuple(trip))
    return out


def _split_heads(x, b, t):
    return x.reshape(1, b, t, -1, LANES)


def kernel(x_prompt, x_sample, cache_a_k, cache_a_v, cache_a_kidx, cache_b_k, cache_b_v, c_prompt, c_sample, rel_bias, w_ada, b_ada, g_pre, g_post, ffn1_w_gate, ffn1_w_up, ffn1_w_down, ffn2_w_gate, ffn2_w_up, ffn2_w_down, w_in, w_out, lambda_q1, lambda_k1, lambda_q2, lambda_k2, g_subln):
    depth = w_in.shape[0]
    assert depth == 1, "stacked caches are returned per layer; only one layer is staged"
    layer = 0
    bp, tp, d = x_prompt.shape
    bs, ts, _ = x_sample.shape
    past = cache_a_k.shape[2]
    assert d == D_MODEL and tp % TQ == 0 and ts <= 16 and past % min(SAMPLE_TK, past) == 0
    lam_init = 0.8 - 0.6 * math.exp(-0.3 * layer)

    n_c = bp + bs
    rows = -(-n_c // 8) * 8
    c_all = jnp.concatenate([c_prompt, c_sample, jnp.zeros((rows - n_c, d), _F32)], axis=0)
    mod = _ada(c_all, w_ada[layer], b_ada[layer])
    mods_p = _mods(mod[:bp], 0)
    mods_s = _mods(mod[bp:bp + bs], ts)

    ffn1 = _pack_ffn(ffn1_w_gate[layer], ffn1_w_up[layer], ffn1_w_down[layer])
    ffn2 = _pack_ffn(ffn2_w_gate[layer], ffn2_w_up[layer], ffn2_w_down[layer])
    w_packed = _pack_w_in(w_in[layer])
    w_out_bf = w_out[layer].astype(_BF16)
    gp, gq = g_pre[layer], g_post[layer]
    lam_vecs = jnp.stack([lambda_q1[layer], lambda_k1[layer], lambda_q2[layer], lambda_k2[layer]], axis=0)
    rel_bias_t = rel_bias.T
    gsub = g_subln[layer]

    (sh, sc, gt) = mods_p[0]
    x1 = _ffn(x_prompt, sh, sc, gt, gp[0], gq[0], *ffn1, per_token=False, res_w=FFN_RES)
    (sh, sc, gt) = mods_p[1]
    (qa, qi, qb1, qb2, ka, va, kw, kb, vb, kabf, vat, kwbf, kwt, kbbf, vbt) = _inproj(
        x1, sh, sc, gp[1], w_packed, per_token=False, transposed=True)
    bias_p = _bias_tiles(rel_bias_t, 2 * TK, TQ, q_on_cols=True, q0=TK, k0=0, n_keys=2 * TK)
    o = _attn_prompt(qa, qi, qb1, qb2, kwt, kwbf, bias_p, lam_vecs, gsub, kabf, vat, kbbf, vbt, lam_init=lam_init)
    x2 = _outproj(o, x1, gt, gq[1], w_out_bf, per_token=False)
    (sh, sc, gt) = mods_p[2]
    y_prompt = _ffn(x2, sh, sc, gt, gp[2], gq[2], *ffn2, per_token=False, res_w=FFN_RES)
    rows_p = (_split_heads(ka, bp, tp), _split_heads(va, bp, tp), kw[..., :IDX_DIM].reshape(1, bp, tp, IDX_DIM),
              _split_heads(kb, bp, tp), _split_heads(vb, bp, tp))

    xs = x_sample.reshape(1, bs * ts, d)
    (sh, sc, gt) = mods_s[0]
    xs1 = _ffn(xs, sh, sc, gt, gp[0], gq[0], *ffn1, per_token=True, res_w=FFN_RES)
    (sh, sc, gt) = mods_s[1]
    (sqa, sqi, sqb1, sqb2, ska, sva, skw, skb, svb, skabf, svabf, skbbf, svbbf) = _inproj(
        xs1, sh, sc, gp[1], w_packed, per_token=True, transposed=False)

    def per_seq(a):
        return a.reshape(bs, ts, a.shape[-1])

    def pad_new(a):
        return jnp.pad(per_seq(a), ((0, 0), (0, NEW_PAD - ts), (0, 0)))

    tk_s = min(SAMPLE_TK, past)
    n_keys = past + ts
    bias_sp = _bias_tiles(rel_bias_t, ts, tk_s, q_on_cols=False, q0=past, k0=past - tk_s, n_keys=n_keys)
    bias_sn = _bias_tiles(rel_bias_t, ts, NEW_PAD, q_on_cols=False, q0=past, k0=past, n_keys=n_keys)
    o_s = _attn_sample(per_seq(sqa), per_seq(sqi), per_seq(sqb1), per_seq(sqb2),
                       pad_new(skw), pad_new(skabf), pad_new(svabf), pad_new(skbbf), pad_new(svbbf),
                       bias_sp, bias_sn, lam_vecs, gsub,
                       cache_a_kidx[layer], cache_a_k[layer].reshape(bs, past * A_HEADS, LANES),
                       cache_a_v[layer].reshape(bs, past * A_HEADS, LANES),
                       cache_b_k[layer].reshape(bs, past * B_HEADS, LANES),
                       cache_b_v[layer].reshape(bs, past * B_HEADS, LANES), lam_init=lam_init)
    xs2 = _outproj(o_s.reshape(1, bs * ts, D_MIX), xs1, gt, gq[1], w_out_bf, per_token=True)
    (sh, sc, gt) = mods_s[2]
    y_sample = _ffn(xs2, sh, sc, gt, gp[2], gq[2], *ffn2, per_token=True, res_w=FFN_RES).reshape(bs, ts, d)
    rows_s = (_split_heads(ska, bs, ts), _split_heads(sva, bs, ts), skw[..., :IDX_DIM].reshape(1, bs, ts, IDX_DIM),
              _split_heads(skb, bs, ts), _split_heads(svb, bs, ts))

    return (y_prompt, y_sample) + rows_p + rows_s
```

```python
import functools
import math

import jax
import jax.numpy as jnp
from jax import lax
from jax.experimental import pallas as pl
from jax.experimental.pallas import tpu as pltpu

D_MODEL = 1024
CHUNK = 64
A_HEADS = 4
A_HEAD_DIM = 128
IDX_HEADS = 8
IDX_DIM = 64
TOPK_MAX = 256
B_HEADS = 4
B_QK_DIM = 64
B_V_DIM = 128
D_MIX = A_HEADS * A_HEAD_DIM + B_HEADS * B_V_DIM
D_FF = 2816
FFN_RES = 0.5
N_BUCKETS = 32
MAX_DISTANCE = 128
N_SUB = 3
RMS_EPS = 1e-6
SUBLN_EPS = 1e-5
IN_SIZES = (512, 512, 512, 512, 64, 8, 512, 512, 512)

LANES = 128
VMEM_LIMIT_BYTES = 56 * 1024 * 1024

FF_CHUNK = 256
TOKEN_TILE = 512
TQ = 256
TK = 256
SAMPLE_TK = 1024
NEW_PAD = 128
QK_LOOKAHEAD = 12
COUNT_ROWS = 32
PASSES_PER_STOP_TEST = 4

NEG = -1e30
LOG2E = math.log2(math.e)
A_SCALE = A_HEAD_DIM ** -0.5 * LOG2E
B_SCALE = B_QK_DIM ** -0.5 * LOG2E
ONES_ROWS = 16
IDX_SCALE = IDX_DIM ** -0.5
IDX_W_SCALE = IDX_HEADS ** -0.5
INT_MIN = -(2 ** 31)

SEG_QA, SEG_QI, SEG_QB1, SEG_QB2, SEG_KA, SEG_VA, SEG_KW, SEG_KB, SEG_VB = (
    (0, 512), (512, 1024), (1536, 512), (2048, 512), (2560, 512), (3072, 512), (3584, 128), (3712, 512), (4224, 512))
N_PACKED = 4736

_NT = (((1,), (1,)), ((), ()))
_F32 = jnp.float32
_BF16 = jnp.bfloat16


def _cparams(sem):
    return pltpu.CompilerParams(dimension_semantics=sem, vmem_limit_bytes=VMEM_LIMIT_BYTES)


def _resident(shape, index_map):
    return pl.BlockSpec(shape, index_map, pipeline_mode=pl.Buffered(1))


def _ada_kernel(c_ref, w_ref, b_ref, o_ref):
    c = c_ref[...]
    s = c * jax.nn.sigmoid(c)
    o_ref[...] = jnp.dot(s, w_ref[...], precision=lax.Precision.HIGHEST,
                         preferred_element_type=_F32) + b_ref[...]


def _ada(c, w_ada, b_ada):
    rows, d = c.shape
    n = w_ada.shape[1]
    tn = 1024
    return pl.pallas_call(
        _ada_kernel,
        grid=(n // tn,),
        in_specs=[pl.BlockSpec((rows, d), lambda j: (0, 0)),
                  pl.BlockSpec((d, tn), lambda j: (0, j)),
                  pl.BlockSpec((1, tn), lambda j: (0, j))],
        out_specs=pl.BlockSpec((rows, tn), lambda j: (0, j)),
        out_shape=jax.ShapeDtypeStruct((rows, n), _F32),
        compiler_params=_cparams(("parallel",)),
        name="ada_mod",
    )(c, w_ada, b_ada.reshape(1, n))


def _rms(x, g, eps):
    return x * lax.rsqrt(jnp.mean(x * x, axis=-1, keepdims=True) + eps) * g


def _mod_specs(per_token, tm, d):
    if per_token:
        return pl.BlockSpec((None, tm, d), lambda b, t: (b, t, 0))
    return pl.BlockSpec((None, 1, d), lambda b, t: (b, 0, 0))


def _ffn_kernel(x_ref, sh_ref, sc_ref, gt_ref, gpre_ref, gpost_ref, wg_ref, wu_ref, wd_ref,
                o_ref, h_ref, acc_ref, *, n_chunks, res_w):
    x = x_ref[...]
    h = _rms(x, gpre_ref[...], RMS_EPS) * (1.0 + sc_ref[...]) + sh_ref[...]
    h_ref[...] = h.astype(_BF16)
    acc_ref[...] = jnp.zeros_like(acc_ref)

    def body(c, carry):
        hb = h_ref[...]
        g = jnp.dot(hb, wg_ref[c], preferred_element_type=_F32)
        u = jnp.dot(hb, wu_ref[c], preferred_element_type=_F32)
        a = (g * jax.nn.sigmoid(g)) * u
        acc_ref[...] += jnp.dot(a.astype(_BF16), wd_ref[c], preferred_element_type=_F32)
        return carry

    lax.fori_loop(0, n_chunks, body, 0)
    y = _rms(acc_ref[...], gpost_ref[...], RMS_EPS)
    o_ref[...] = x + res_w * gt_ref[...] * y


def _ffn(x, shift, scale, gate, g_pre, g_post, wg3, wu3, wd3, *, per_token, res_w):
    bsz, t, d = x.shape
    tm = min(TOKEN_TILE, t)
    n_chunks = wg3.shape[0]
    tok = pl.BlockSpec((None, tm, d), lambda b, i: (b, i, 0))
    vec = pl.BlockSpec((1, d), lambda b, i: (0, 0))
    mod = _mod_specs(per_token, tm, d)
    return pl.pallas_call(
        functools.partial(_ffn_kernel, n_chunks=n_chunks, res_w=res_w),
        grid=(bsz, t // tm),
        in_specs=[tok, mod, mod, mod, vec, vec,
                  _resident(wg3.shape, lambda b, i: (0, 0, 0)),
                  _resident(wu3.shape, lambda b, i: (0, 0, 0)),
                  _resident(wd3.shape, lambda b, i: (0, 0, 0))],
        out_specs=tok,
        out_shape=jax.ShapeDtypeStruct(x.shape, _F32),
        scratch_shapes=[pltpu.VMEM((tm, d), _BF16), pltpu.VMEM((tm, d), _F32)],
        compiler_params=_cparams(("parallel", "parallel")),
        name="ffn",
    )(x, shift, scale, gate, g_pre.reshape(1, d), g_post.reshape(1, d), wg3, wu3, wd3)


def _inproj_kernel(x_ref, sh_ref, sc_ref, gpre_ref, w_ref, *out_refs, transposed):
    if transposed:
        (qa_ref, qi_ref, qb1_ref, qb2_ref, ka_ref, va_ref, kw_ref, kb_ref, vb_ref,
         kabf_ref, vat_ref, kwbf_ref, kwt_ref, kbbf_ref, vbt_ref) = out_refs
    else:
        (qa_ref, qi_ref, qb1_ref, qb2_ref, ka_ref, va_ref, kw_ref, kb_ref, vb_ref,
         kabf_ref, vabf_ref, kbbf_ref, vbbf_ref) = out_refs
    h = _rms(x_ref[...], gpre_ref[...], RMS_EPS) * (1.0 + sc_ref[...]) + sh_ref[...]
    hb = h.astype(_BF16)

    def proj(seg):
        off, width = seg
        return jnp.dot(hb, w_ref[:, off:off + width], preferred_element_type=_F32)

    qa_ref[...] = (proj(SEG_QA) * A_SCALE).astype(_BF16)
    qi_ref[...] = (proj(SEG_QI) * IDX_SCALE).astype(_BF16)
    qb1_ref[...] = (proj(SEG_QB1) * B_SCALE).astype(_BF16)
    qb2_ref[...] = (proj(SEG_QB2) * B_SCALE).astype(_BF16)
    tm = x_ref.shape[0]

    def store_rows(ref, val, heads):
        for hd in range(heads):
            ref[pl.ds(hd, tm, stride=heads), :] = val[:, hd * LANES:(hd + 1) * LANES]

    ka = proj(SEG_KA)
    store_rows(ka_ref, ka, A_HEADS)
    kabf_ref[...] = ka.astype(_BF16)
    kb = proj(SEG_KB)
    store_rows(kb_ref, kb, B_HEADS)
    kbbf_ref[...] = kb.astype(_BF16)
    kw = proj(SEG_KW)
    kw_ref[...] = kw
    va = proj(SEG_VA)
    store_rows(va_ref, va, A_HEADS)
    vb = proj(SEG_VB)
    store_rows(vb_ref, vb, B_HEADS)
    if transposed:
        kwbf_ref[...] = kw.astype(_BF16)
        kwt_ref[...] = kw.T
        vat_ref[...] = va.T.astype(_BF16)
        vbt_ref[...] = vb.T.astype(_BF16)
    else:
        vabf_ref[...] = va.astype(_BF16)
        vbbf_ref[...] = vb.astype(_BF16)


def _inproj(x, shift, scale, g_pre, w_packed, *, per_token, transposed):
    bsz, t, d = x.shape
    tm = min(TOKEN_TILE, t)

    def tok(width, dtype):
        return (pl.BlockSpec((None, tm, width), lambda b, i: (b, i, 0)),
                jax.ShapeDtypeStruct((bsz, t, width), dtype))

    def tr(width, dtype):
        return (pl.BlockSpec((None, width, tm), lambda b, i: (b, 0, i)),
                jax.ShapeDtypeStruct((bsz, width, t), dtype))

    def rows(heads):
        return (pl.BlockSpec((None, tm * heads, LANES), lambda b, i: (b, i, 0)),
                jax.ShapeDtypeStruct((bsz, t * heads, LANES), _F32))

    outs = [tok(512, _BF16), tok(1024, _BF16), tok(512, _BF16), tok(512, _BF16),
            rows(A_HEADS), rows(A_HEADS), tok(LANES, _F32), rows(B_HEADS), rows(B_HEADS)]
    if transposed:
        outs += [tok(512, _BF16), tr(512, _BF16), tok(LANES, _BF16), tr(LANES, _F32),
                 tok(512, _BF16), tr(512, _BF16)]
    else:
        outs += [tok(512, _BF16), tok(512, _BF16), tok(512, _BF16), tok(512, _BF16)]
    mod = _mod_specs(per_token, tm, d)
    return pl.pallas_call(
        functools.partial(_inproj_kernel, transposed=transposed),
        grid=(bsz, t // tm),
        in_specs=[pl.BlockSpec((None, tm, d), lambda b, i: (b, i, 0)), mod, mod,
                  pl.BlockSpec((1, d), lambda b, i: (0, 0)),
                  _resident(w_packed.shape, lambda b, i: (0, 0))],
        out_specs=[o[0] for o in outs],
        out_shape=[o[1] for o in outs],
        compiler_params=_cparams(("parallel", "parallel")),
        name="inproj",
    )(x, shift, scale, g_pre.reshape(1, d), w_packed)


def _bucket_thresholds():
    nb = N_BUCKETS // 2
    max_exact = nb // 2
    ratio = MAX_DISTANCE / max_exact
    out = []
    for j in range(1, nb - max_exact):
        out.append(int(math.ceil(max_exact * ratio ** (j / (nb - max_exact)) - 1e-9)))
    return nb, max_exact, tuple(out)


def _bias_tile_kernel(tab_ref, o_ref, *, q_on_cols, q0, k0, n_keys):
    hh = pl.program_id(0)
    shape = o_ref.shape
    r = lax.broadcasted_iota(jnp.int32, shape, 0)
    c = lax.broadcasted_iota(jnp.int32, shape, 1)
    qpos = q0 + (c if q_on_cols else r)
    kpos = k0 + (r if q_on_cols else c)
    rel = kpos - qpos
    n = jnp.abs(rel)
    nb, max_exact, thresholds = _bucket_thresholds()
    large = jnp.full(shape, max_exact, jnp.int32)
    for th in thresholds:
        large = large + jnp.where(n >= th, 1, 0)
    bucket = jnp.where(rel > 0, nb, 0) + jnp.where(n < max_exact, n, large)
    val = jnp.zeros(shape, _F32)
    for bkt in range(N_BUCKETS):
        val = jnp.where(bucket == bkt, tab_ref[hh, bkt], val)
    val = (val - tab_ref[hh, nb - 1]) * LOG2E
    chunk_end = (lax.shift_right_arithmetic(qpos, int(math.log2(CHUNK))) + 1) * CHUNK
    admit = (kpos < chunk_end) & (kpos >= 0) & (kpos < n_keys)
    o_ref[...] = jnp.where(admit, val, NEG)


def _bias_tiles(rel_bias_t, rows, cols, *, q_on_cols, q0, k0, n_keys):
    n_heads = rel_bias_t.shape[0]
    return pl.pallas_call(
        functools.partial(_bias_tile_kernel, q_on_cols=q_on_cols, q0=q0, k0=k0, n_keys=n_keys),
        grid=(n_heads,),
        in_specs=[pl.BlockSpec(memory_space=pltpu.SMEM)],
        out_specs=pl.BlockSpec((None, rows, cols), lambda h: (h, 0, 0)),
        out_shape=jax.ShapeDtypeStruct((n_heads, rows, cols), _F32),
        compiler_params=_cparams(("parallel",)),
        name="bias_tiles",
    )(rel_bias_t)


def _order_key(score):
    bits = lax.bitcast_convert_type(score, jnp.int32)
    return bits ^ (lax.shift_right_arithmetic(bits, 31) & 0x7FFFFFFF)


def _select_threshold(count_ge, shape, n_sel, n_total):
    def cond(carry):
        it, _, _, worst = carry
        return (it < 32) & (worst > n_sel)

    def one_pass(it, state):
        t_u, c_t = state
        bit = lax.shift_left(jnp.int32(1), 31 - it)
        cand_u = t_u | bit
        cnt = count_ge(cand_u ^ INT_MIN)
        take = cnt >= n_sel
        return jnp.where(take, cand_u, t_u), jnp.where(take, cnt, c_t)

    def body(carry):
        it, t_u, c_t, _ = carry
        t_u, c_t = lax.fori_loop(0, PASSES_PER_STOP_TEST, lambda j, st: one_pass(it + j, st), (t_u, c_t))
        return it + PASSES_PER_STOP_TEST, t_u, c_t, jnp.max(c_t)

    n_total = jnp.asarray(n_total, jnp.int32)
    init = (jnp.int32(0), jnp.zeros(shape, jnp.int32), jnp.full(shape, n_total, jnp.int32), n_total)
    _, t_u, _, _ = lax.while_loop(cond, body, init)
    return t_u ^ INT_MIN


def _lambda(lam_ref, lam_init):
    l1 = jnp.sum(lam_ref[0:1, :] * lam_ref[1:2, :], axis=-1, keepdims=True)
    l2 = jnp.sum(lam_ref[2:3, :] * lam_ref[3:4, :], axis=-1, keepdims=True)
    return jnp.exp(l1) - jnp.exp(l2) + lam_init


def _attn_prompt_kernel(qa_ref, qi_ref, qb1_ref, qb2_ref, kwt_ref, kw_ref, bias_ref, lam_ref, gsub_ref,
                        ka_hbm, vat_hbm, kb_hbm, vbt_hbm, o_ref,
                        s_ref, kabuf, vabuf, kbbuf, vbbuf, sem, acc_ref, m_ref, l_ref,
                        *, n_sel, lam_init):
    b = pl.program_id(0)
    i = pl.program_id(1)

    def copies(c, slot):
        k0 = pl.multiple_of(c * TK, TK)
        vrows = pl.ds(0, B_V_DIM)
        return (pltpu.make_async_copy(ka_hbm.at[b, pl.ds(k0, TK), :], kabuf.at[slot], sem.at[0, slot]),
                pltpu.make_async_copy(vat_hbm.at[b, :, :, pl.ds(k0, TK)], vabuf.at[slot, :, vrows, :], sem.at[1, slot]),
                pltpu.make_async_copy(kb_hbm.at[b, pl.ds(k0, TK), :], kbbuf.at[slot], sem.at[2, slot]),
                pltpu.make_async_copy(vbt_hbm.at[b, :, :, pl.ds(k0, TK)], vbbuf.at[slot, :, vrows, :], sem.at[3, slot]))

    def start(c, slot):
        for cp in copies(c, slot):
            cp.start()

    def wait(c, slot):
        for cp in copies(c, slot):
            cp.wait()

    start(0, 0)
    ones = jnp.ones((2, A_HEADS, ONES_ROWS, TK), _BF16)
    vabuf[:, :, B_V_DIM:, :] = ones
    vbbuf[:, :, B_V_DIM:, :] = ones

    w_rows = kwt_ref[IDX_DIM:IDX_DIM + IDX_HEADS, :] * IDX_W_SCALE
    rr = lax.broadcasted_iota(jnp.int32, (TK, TQ), 0)
    cc = lax.broadcasted_iota(jnp.int32, (TK, TQ), 1)
    admit_diag = rr < (lax.shift_right_arithmetic(cc, int(math.log2(CHUNK))) + 1) * CHUNK

    def score_chunk(c, diagonal):
        k0 = pl.multiple_of(c * TK, TK)
        kblk = kw_ref[pl.ds(k0, TK), :]
        acc = None
        for h in range(IDX_HEADS):
            d = lax.dot_general(kblk, qi_ref[:, h * LANES:(h + 1) * LANES], _NT, preferred_element_type=_F32)
            term = jnp.maximum(d, 0.0) * w_rows[h:h + 1, :]
            acc = term if acc is None else acc + term
        if diagonal:
            acc = jnp.where(admit_diag, acc, -jnp.inf)
        s_ref[pl.ds(k0, TK), :] = _order_key(acc)

    def far_scores(c, carry):
        score_chunk(c, False)
        return carry

    lax.fori_loop(0, i, far_scores, 0)
    score_chunk(i, True)

    def count_ge(cand):
        def hits(k0, rows, one):
            hit = jnp.where(s_ref[pl.ds(k0, rows), :] >= cand, one, 0)
            return jnp.sum(hit.reshape(rows // COUNT_ROWS, COUNT_ROWS, TQ), axis=0)

        def body(c, acc):
            return acc + hits(pl.multiple_of(c * (2 * TK), 2 * TK), 2 * TK, 1)

        n_chunks = i + 1
        part = lax.fori_loop(0, lax.shift_right_logical(n_chunks, 1), body,
                             jnp.zeros((COUNT_ROWS, TQ), jnp.int32))
        part = part + hits(pl.multiple_of(i * TK, TK), TK, n_chunks & 1)
        return jnp.sum(part, axis=0, keepdims=True)

    t_row = _select_threshold(count_ge, (1, TQ), n_sel, (i + 1) * TK)

    m_ref[...] = jnp.full(m_ref.shape, NEG, _F32)
    l_ref[...] = jnp.zeros_like(l_ref)
    acc_ref[...] = jnp.zeros_like(acc_ref)

    def logits(k_sl, q_sl):
        return lax.dot_general(k_sl, q_sl, _NT, preferred_element_type=_F32)

    def accumulate(idx, s, v_sl, bias, sel):
        if bias is not None:
            s = s + bias
        if sel is not None:
            s = jnp.where(sel, s, NEG)
        m_prev = m_ref[idx:idx + 1, :]
        m_new = jnp.maximum(m_prev, jnp.max(s, axis=0, keepdims=True))
        p = jnp.exp2(s - m_new)
        alpha = jnp.exp2(m_prev - m_new)
        pv = jnp.dot(v_sl, p.astype(_BF16), preferred_element_type=_F32)
        l_ref[idx:idx + 1, :] = alpha * l_ref[idx:idx + 1, :] + pv[B_V_DIM:B_V_DIM + 1, :]
        acc_ref[idx] = alpha * acc_ref[idx] + pv[:B_V_DIM, :]
        m_ref[idx:idx + 1, :] = m_new

    def process(c, slot, bias_half):
        k0 = pl.multiple_of(c * TK, TK)
        sel = s_ref[pl.ds(k0, TK), :] >= t_row
        rows = None if bias_half is None else slice(bias_half * TK, (bias_half + 1) * TK)
        streams = []
        for h in range(A_HEADS):
            streams.append((h, kabuf, qa_ref, vabuf, h, h, sel))
        for h in range(B_HEADS):
            for half, q_ref in enumerate((qb1_ref, qb2_ref)):
                streams.append((A_HEADS + 2 * h + half, kbbuf, q_ref, vbbuf, h, A_HEADS + h, None))

        def qk(st):
            _, kbuf, q_ref, _, h, _, _ = st
            hs = slice(h * LANES, (h + 1) * LANES)
            return logits(kbuf[slot, :, hs], q_ref[:, hs])

        pending = [qk(st) for st in streams[:QK_LOOKAHEAD]]
        for n, st in enumerate(streams):
            idx, _, _, vbuf, h, bias_head, st_sel = st
            s_cur = pending.pop(0)
            if n + QK_LOOKAHEAD < len(streams):
                pending.append(qk(streams[n + QK_LOOKAHEAD]))
            bias = None if rows is None else bias_ref[bias_head, rows, :]
            accumulate(idx, s_cur, vbuf[slot, h], bias, st_sel)

    def far_body(c, carry):
        slot = lax.rem(c, 2)
        wait(c, slot)
        start(c + 1, 1 - slot)
        process(c, slot, None)
        return carry

    lax.fori_loop(0, jnp.maximum(i - 1, 0), far_body, 0)

    @pl.when(i >= 1)
    def _():
        c = i - 1
        slot = lax.rem(c, 2)
        wait(c, slot)
        start(i, 1 - slot)
        process(c, slot, 0)

    slot = lax.rem(i, 2)
    wait(i, slot)
    process(i, slot, 1)

    for h in range(A_HEADS):
        o_t = acc_ref[h] / l_ref[h:h + 1, :]
        o_ref[:, h * LANES:(h + 1) * LANES] = o_t.T.astype(o_ref.dtype)
    lam = _lambda(lam_ref, lam_init)
    for h in range(B_HEADS):
        i1 = A_HEADS + 2 * h
        d = acc_ref[i1] / l_ref[i1:i1 + 1, :] - lam * (acc_ref[i1 + 1] / l_ref[i1 + 1:i1 + 2, :])
        y = d * lax.rsqrt(jnp.mean(d * d, axis=0, keepdims=True) + SUBLN_EPS) * gsub_ref[...]
        y = y * (1.0 - lam_init)
        off = A_HEADS * A_HEAD_DIM + h * B_V_DIM
        o_ref[:, off:off + B_V_DIM] = y.T.astype(o_ref.dtype)


def _attn_prompt(qa, qi, qb1, qb2, kwt, kwbf, bias, lam_vecs, g_subln, kabf, vat, kbbf, vbt, *, lam_init):
    bsz, t, _ = qa.shape
    n_sel = min(TOPK_MAX, t // 4)
    n_streams = A_HEADS + 2 * B_HEADS

    def qspec(width):
        return pl.BlockSpec((None, TQ, width), lambda b, i: (b, i, 0))

    hbm = pl.BlockSpec(memory_space=pl.ANY)
    return pl.pallas_call(
        functools.partial(_attn_prompt_kernel, n_sel=n_sel, lam_init=lam_init),
        grid=(bsz, t // TQ),
        in_specs=[qspec(512), qspec(1024), qspec(512), qspec(512),
                  pl.BlockSpec((None, LANES, TQ), lambda b, i: (b, 0, i)),
                  pl.BlockSpec((None, t, LANES), lambda b, i: (b, 0, 0)),
                  _resident(bias.shape, lambda b, i: (0, 0, 0)),
                  pl.BlockSpec(lam_vecs.shape, lambda b, i: (0, 0)),
                  pl.BlockSpec((B_V_DIM, 1), lambda b, i: (0, 0)),
                  hbm, hbm, hbm, hbm],
        out_specs=pl.BlockSpec((None, TQ, D_MIX), lambda b, i: (b, i, 0)),
        out_shape=jax.ShapeDtypeStruct((bsz, t, D_MIX), _BF16),
        scratch_shapes=[pltpu.VMEM((t, TQ), jnp.int32),
                        pltpu.VMEM((2, TK, 512), _BF16), pltpu.VMEM((2, A_HEADS, B_V_DIM + ONES_ROWS, TK), _BF16),
                        pltpu.VMEM((2, TK, 512), _BF16), pltpu.VMEM((2, B_HEADS, B_V_DIM + ONES_ROWS, TK), _BF16),
                        pltpu.SemaphoreType.DMA((4, 2)),
                        pltpu.VMEM((n_streams, B_V_DIM, TQ), _F32),
                        pltpu.VMEM((16, TQ), _F32), pltpu.VMEM((16, TQ), _F32)],
        compiler_params=_cparams(("arbitrary", "arbitrary")),
        name="attn_prompt",
    )(qa, qi, qb1, qb2, kwt, kwbf, bias, lam_vecs, g_subln.reshape(B_V_DIM, 1),
      kabf, vat.reshape(bsz, A_HEADS, A_HEAD_DIM, t), kbbf, vbt.reshape(bsz, B_HEADS, B_V_DIM, t))


def _attn_sample_kernel(qa_ref, qi_ref, qb1_ref, qb2_ref, kwn_ref, kan_ref, van_ref, kbn_ref, vbn_ref,
                        bias_p_ref, bias_n_ref, lam_ref, gsub_ref,
                        cidx_ref, cka_ref, cva_ref, ckb_ref, cvb_ref, o_ref,
                        sp_ref, sn_ref, t_ref, acc_ref, m_ref, l_ref,
                        *, n_sel, lam_init, n_new, nk):
    k = pl.program_id(1)
    tk = cka_ref.shape[0] // A_HEADS

    def idx_scores(kidx_bf):
        acc = None
        for h in range(IDX_HEADS):
            q_h = qi_ref[:, h * LANES:h * LANES + IDX_DIM]
            d = lax.dot_general(q_h, kidx_bf, _NT, preferred_element_type=_F32)
            w_h = kwn_ref[0:n_new, IDX_DIM + h:IDX_DIM + h + 1] * IDX_W_SCALE
            term = jnp.maximum(d, 0.0) * w_h
            acc = term if acc is None else acc + term
        return acc

    @pl.when(k == 0)
    def _():
        for j in range(nk):
            kidx = cidx_ref[j * tk:(j + 1) * tk, :].astype(_BF16)
            sp_ref[j] = _order_key(idx_scores(kidx))
        s_new = idx_scores(kwn_ref[:, 0:IDX_DIM].astype(_BF16))
        col = lax.broadcasted_iota(jnp.int32, s_new.shape, 1)
        sn_ref[...] = _order_key(jnp.where(col < n_new, s_new, -jnp.inf))

        def count_ge(cand):
            tot = jnp.sum(jnp.where(sn_ref[...] >= cand, 1, 0), axis=1, keepdims=True)
            for j in range(nk):
                tot = tot + jnp.sum(jnp.where(sp_ref[j] >= cand, 1, 0), axis=1, keepdims=True)
            return tot

        t_ref[...] = _select_threshold(count_ge, (n_new, 1), n_sel, nk * tk + NEW_PAD)
        m_ref[...] = jnp.full(m_ref.shape, NEG, _F32)
        l_ref[...] = jnp.zeros_like(l_ref)
        acc_ref[...] = jnp.zeros_like(acc_ref)

    def stream(idx, q_sl, k_sl, v_sl, bias, sel):
        s = lax.dot_general(q_sl, k_sl, _NT, preferred_element_type=_F32)
        if bias is not None:
            s = s + bias
        if sel is not None:
            s = jnp.where(sel, s, NEG)
        m_prev = m_ref[idx]
        m_new = jnp.maximum(m_prev, jnp.max(s, axis=1, keepdims=True))
        p = jnp.exp2(s - m_new)
        alpha = jnp.exp2(m_prev - m_new)
        l_ref[idx] = alpha * l_ref[idx] + jnp.sum(p, axis=1, keepdims=True)
        acc_ref[idx] = alpha * acc_ref[idx] + jnp.dot(p.astype(_BF16), v_sl, preferred_element_type=_F32)
        m_ref[idx] = m_new

    def process(head_rows, keys, bias_ref):
        sel = keys >= t_ref[...]
        for h in range(A_HEADS):
            hs = slice(h * LANES, (h + 1) * LANES)
            bias = None if bias_ref is None else bias_ref[h]
            stream(h, qa_ref[:, hs], head_rows(0, h), head_rows(1, h), bias, sel)
        for h in range(B_HEADS):
            hs = slice(h * LANES, (h + 1) * LANES)
            bias = None if bias_ref is None else bias_ref[A_HEADS + h]
            kb, vb = head_rows(2, h), head_rows(3, h)
            for half, q_ref in enumerate((qb1_ref, qb2_ref)):
                stream(A_HEADS + 2 * h + half, q_ref[:, hs], kb, vb, bias, None)

    def cached_rows(which, h):
        ref = (cka_ref, cva_ref, ckb_ref, cvb_ref)[which]
        return ref[pl.ds(h, tk, stride=A_HEADS), :].astype(_BF16)

    def new_rows(which, h):
        ref = (kan_ref, van_ref, kbn_ref, vbn_ref)[which]
        return ref[:, h * LANES:(h + 1) * LANES]

    @pl.when(k < nk - 1)
    def _():
        process(cached_rows, sp_ref[k], None)

    @pl.when(k == nk - 1)
    def _():
        process(cached_rows, sp_ref[k], bias_p_ref)
        process(new_rows, sn_ref[...], bias_n_ref)
        for h in range(A_HEADS):
            o_ref[:, h * LANES:(h + 1) * LANES] = (acc_ref[h] / l_ref[h]).astype(o_ref.dtype)
        lam = _lambda(lam_ref, lam_init)
        for h in range(B_HEADS):
            i1 = A_HEADS + 2 * h
            d = acc_ref[i1] / l_ref[i1] - lam * (acc_ref[i1 + 1] / l_ref[i1 + 1])
            y = d * lax.rsqrt(jnp.mean(d * d, axis=1, keepdims=True) + SUBLN_EPS) * gsub_ref[...]
            y = y * (1.0 - lam_init)
            off = A_HEADS * A_HEAD_DIM + h * B_V_DIM
            o_ref[:, off:off + B_V_DIM] = y.astype(o_ref.dtype)


def _attn_sample(qa, qi, qb1, qb2, kw_new, ka_new, va_new, kb_new, vb_new, bias_p, bias_n, lam_vecs, g_subln,
                 c_idx, c_ka, c_va, c_kb, c_vb, *, lam_init):
    bsz, n_new, _ = qa.shape
    past = c_idx.shape[1]
    tk = min(SAMPLE_TK, past)
    nk = past // tk
    n_sel = min(TOPK_MAX, (past + n_new) // 4)
    n_streams = A_HEADS + 2 * B_HEADS

    def per_b(rows, width):
        return pl.BlockSpec((None, rows, width), lambda b, k: (b, 0, 0))

    def cache():
        return pl.BlockSpec((None, tk * A_HEADS, LANES), lambda b, k: (b, k, 0))

    def const(shape):
        return pl.BlockSpec(shape, lambda b, k: (0,) * len(shape))

    return pl.pallas_call(
        functools.partial(_attn_sample_kernel, n_sel=n_sel, lam_init=lam_init, n_new=n_new, nk=nk),
        grid=(bsz, nk),
        in_specs=[per_b(n_new, 512), per_b(n_new, 1024), per_b(n_new, 512), per_b(n_new, 512),
                  per_b(NEW_PAD, LANES), per_b(NEW_PAD, 512), per_b(NEW_PAD, 512), per_b(NEW_PAD, 512),
                  per_b(NEW_PAD, 512),
                  const(bias_p.shape), const(bias_n.shape), const(lam_vecs.shape), const((1, B_V_DIM)),
                  per_b(past, IDX_DIM), cache(), cache(), cache(), cache()],
        out_specs=per_b(n_new, D_MIX),
        out_shape=jax.ShapeDtypeStruct((bsz, n_new, D_MIX), _BF16),
        scratch_shapes=[pltpu.VMEM((nk, n_new, tk), jnp.int32), pltpu.VMEM((n_new, NEW_PAD), jnp.int32),
                        pltpu.VMEM((n_new, 1), jnp.int32),
                        pltpu.VMEM((n_streams, n_new, B_V_DIM), _F32),
                        pltpu.VMEM((n_streams, n_new, 1), _F32), pltpu.VMEM((n_streams, n_new, 1), _F32)],
        compiler_params=_cparams(("parallel", "arbitrary")),
        name="attn_sample",
    )(qa, qi, qb1, qb2, kw_new, ka_new, va_new, kb_new, vb_new, bias_p, bias_n, lam_vecs,
      g_subln.reshape(1, B_V_DIM), c_idx, c_ka, c_va, c_kb, c_vb)


def _outproj_kernel(o_ref, x_ref, gt_ref, gpost_ref, w_ref, y_ref):
    mix = jnp.dot(o_ref[...], w_ref[...], preferred_element_type=_F32)
    y_ref[...] = x_ref[...] + gt_ref[...] * _rms(mix, gpost_ref[...], RMS_EPS)


def _outproj(o, x, gate, g_post, w_out_bf, *, per_token):
    bsz, t, d = x.shape
    tm = min(TOKEN_TILE, t)
    tok = pl.BlockSpec((None, tm, d), lambda b, i: (b, i, 0))
    return pl.pallas_call(
        _outproj_kernel,
        grid=(bsz, t // tm),
        in_specs=[pl.BlockSpec((None, tm, D_MIX), lambda b, i: (b, i, 0)), tok, _mod_specs(per_token, tm, d),
                  pl.BlockSpec((1, d), lambda b, i: (0, 0)),
                  _resident(w_out_bf.shape, lambda b, i: (0, 0))],
        out_specs=tok,
        out_shape=jax.ShapeDtypeStruct(x.shape, _F32),
        compiler_params=_cparams(("parallel", "parallel")),
        name="outproj",
    )(o, x, gate, g_post.reshape(1, d), w_out_bf)


def _pack_w_in(w_in):
    d = w_in.shape[0]
    parts, off = [], 0
    for n in IN_SIZES:
        parts.append(w_in[:, off:off + n])
        off += n
    q_a, k_a, v_a, q_idx, k_idx, w_idx, q_b, k_b, v_b = parts
    zi = jnp.zeros((d, IDX_HEADS, LANES - IDX_DIM), w_in.dtype)
    qi = jnp.concatenate([q_idx.reshape(d, IDX_HEADS, IDX_DIM), zi], axis=-1).reshape(d, IDX_HEADS * LANES)
    qb = q_b.reshape(d, B_HEADS, 2, B_QK_DIM)
    zb = jnp.zeros((d, B_HEADS, B_QK_DIM), w_in.dtype)
    qb1 = jnp.concatenate([qb[:, :, 0], zb], axis=-1).reshape(d, B_HEADS * LANES)
    qb2 = jnp.concatenate([zb, qb[:, :, 1]], axis=-1).reshape(d, B_HEADS * LANES)
    kw = jnp.concatenate([k_idx, w_idx, jnp.zeros((d, LANES - IDX_DIM - IDX_HEADS), w_in.dtype)], axis=-1)
    packed = jnp.concatenate([q_a, qi, qb1, qb2, k_a, v_a, kw, k_b, v_b], axis=-1)
    assert packed.shape[1] == N_PACKED
    return packed.astype(_BF16)


def _pack_ffn(w_gate, w_up, w_down):
    d, f = w_gate.shape
    n = f // FF_CHUNK
    wg3 = w_gate.reshape(d, n, FF_CHUNK).transpose(1, 0, 2).astype(_BF16)
    wu3 = w_up.reshape(d, n, FF_CHUNK).transpose(1, 0, 2).astype(_BF16)
    wd3 = w_down.reshape(n, FF_CHUNK, d).astype(_BF16)
    return wg3, wu3, wd3


def _mods(mod_rows, per_token_len):
    rows = mod_rows.shape[0]
    m = mod_rows.reshape(rows, N_SUB, 3, 1, D_MODEL)
    out = []
    for s in range(N_SUB):
        trip = []
        for kind in range(3):
            v = m[:, s, kind]
            if per_token_len:
                v = jnp.broadcast_to(v, (rows, per_token_len, D_MODEL)).reshape(1, rows * per_token_len, D_MODEL)
            trip.append(v)
        out.append(tuple(trip))
    return out


def _split_heads(x, b, t):
    return x.reshape(1, b, t, -1, LANES)


def kernel(x_prompt, x_sample, cache_a_k, cache_a_v, cache_a_kidx, cache_b_k, cache_b_v, c_prompt, c_sample, rel_bias, w_ada, b_ada, g_pre, g_post, ffn1_w_gate, ffn1_w_up, ffn1_w_down, ffn2_w_gate, ffn2_w_up, ffn2_w_down, w_in, w_out, lambda_q1, lambda_k1, lambda_q2, lambda_k2, g_subln):
    depth = w_in.shape[0]
    assert depth == 1, "stacked caches are returned per layer; only one layer is staged"
    layer = 0
    bp, tp, d = x_prompt.shape
    bs, ts, _ = x_sample.shape
    past = cache_a_k.shape[2]
    assert d == D_MODEL and tp % TQ == 0 and ts <= 16 and past % min(SAMPLE_TK, past) == 0
    lam_init = 0.8 - 0.6 * math.exp(-0.3 * layer)

    n_c = bp + bs
    rows = -(-n_c // 8) * 8
    c_all = jnp.concatenate([c_prompt, c_sample, jnp.zeros((rows - n_c, d), _F32)], axis=0)
    mod = _ada(c_all, w_ada[layer], b_ada[layer])
    mods_p = _mods(mod[:bp], 0)
    mods_s = _mods(mod[bp:bp + bs], ts)

    ffn1 = _pack_ffn(ffn1_w_gate[layer], ffn1_w_up[layer], ffn1_w_down[layer])
    ffn2 = _pack_ffn(ffn2_w_gate[layer], ffn2_w_up[layer], ffn2_w_down[layer])
    w_packed = _pack_w_in(w_in[layer])
    w_out_bf = w_out[layer].astype(_BF16)
    gp, gq = g_pre[layer], g_post[layer]
    lam_vecs = jnp.stack([lambda_q1[layer], lambda_k1[layer], lambda_q2[layer], lambda_k2[layer]], axis=0)
    rel_bias_t = rel_bias.T
    gsub = g_subln[layer]

    (sh, sc, gt) = mods_p[0]
    x1 = _ffn(x_prompt, sh, sc, gt, gp[0], gq[0], *ffn1, per_token=False, res_w=FFN_RES)
    (sh, sc, gt) = mods_p[1]
    (qa, qi, qb1, qb2, ka, va, kw, kb, vb, kabf, vat, kwbf, kwt, kbbf, vbt) = _inproj(
        x1, sh, sc, gp[1], w_packed, per_token=False, transposed=True)
    bias_p = _bias_tiles(rel_bias_t, 2 * TK, TQ, q_on_cols=True, q0=TK, k0=0, n_keys=2 * TK)
    o = _attn_prompt(qa, qi, qb1, qb2, kwt, kwbf, bias_p, lam_vecs, gsub, kabf, vat, kbbf, vbt, lam_init=lam_init)
    x2 = _outproj(o, x1, gt, gq[1], w_out_bf, per_token=False)
    (sh, sc, gt) = mods_p[2]
    y_prompt = _ffn(x2, sh, sc, gt, gp[2], gq[2], *ffn2, per_token=False, res_w=FFN_RES)
    rows_p = (_split_heads(ka, bp, tp), _split_heads(va, bp, tp), kw[..., :IDX_DIM].reshape(1, bp, tp, IDX_DIM),
              _split_heads(kb, bp, tp), _split_heads(vb, bp, tp))

    xs = x_sample.reshape(1, bs * ts, d)
    (sh, sc, gt) = mods_s[0]
    xs1 = _ffn(xs, sh, sc, gt, gp[0], gq[0], *ffn1, per_token=True, res_w=FFN_RES)
    (sh, sc, gt) = mods_s[1]
    (sqa, sqi, sqb1, sqb2, ska, sva, skw, skb, svb, skabf, svabf, skbbf, svbbf) = _inproj(
        xs1, sh, sc, gp[1], w_packed, per_token=True, transposed=False)

    def per_seq(a):
        return a.reshape(bs, ts, a.shape[-1])

    def pad_new(a):
        return jnp.pad(per_seq(a), ((0, 0), (0, NEW_PAD - ts), (0, 0)))

    tk_s = min(SAMPLE_TK, past)
    n_keys = past + ts
    bias_sp = _bias_tiles(rel_bias_t, ts, tk_s, q_on_cols=False, q0=past, k0=past - tk_s, n_keys=n_keys)
    bias_sn = _bias_tiles(rel_bias_t, ts, NEW_PAD, q_on_cols=False, q0=past, k0=past, n_keys=n_keys)
    o_s = _attn_sample(per_seq(sqa), per_seq(sqi), per_seq(sqb1), per_seq(sqb2),
                       pad_new(skw), pad_new(skabf), pad_new(svabf), pad_new(skbbf), pad_new(svbbf),
                       bias_sp, bias_sn, lam_vecs, gsub,
                       cache_a_kidx[layer], cache_a_k[layer].reshape(bs, past * A_HEADS, LANES),
                       cache_a_v[layer].reshape(bs, past * A_HEADS, LANES),
                       cache_b_k[layer].reshape(bs, past * B_HEADS, LANES),
                       cache_b_v[layer].reshape(bs, past * B_HEADS, LANES), lam_init=lam_init)
    xs2 = _outproj(o_s.reshape(1, bs * ts, D_MIX), xs1, gt, gq[1], w_out_bf, per_token=True)
    (sh, sc, gt) = mods_s[2]
    y_sample = _ffn(xs2, sh, sc, gt, gp[2], gq[2], *ffn2, per_token=True, res_w=FFN_RES).reshape(bs, ts, d)
    rows_s = (_split_heads(ska, bs, ts), _split_heads(sva, bs, ts), skw[..., :IDX_DIM].reshape(1, bs, ts, IDX_DIM),
              _split_heads(skb, bs, ts), _split_heads(svb, bs, ts))

    return (y_prompt, y_sample) + rows_p + rows_s
```
